```python
import jax, jax.numpy as jnp
from jax import lax
import numpy as np

D_MODEL = 1024
BATCH = 8
SEQ = 4096
DEPTH = 1

GRID_W = 64
CTX_LEN = 256
FOURIER_WIDTH = D_MODEL // 2
FOURIER_GROUPS = 4
FOURIER_GROUP_DIM = FOURIER_WIDTH // FOURIER_GROUPS
LRU_WIDTH = D_MODEL
LRU_HEADS = 8
LRU_HEAD_DIM = LRU_WIDTH // LRU_HEADS
LRU_CONV_W = 4
LRU_C = 8.0
N_DIR = 2
D_FF = 2816
N_MOD = 6
RMS_EPS = 1e-6
IN_WIDTH = FOURIER_WIDTH + 2 * LRU_WIDTH + 2 * D_MODEL

kernel_name = "fourier_rglru_convffn_hybrid_dit"


def rms_norm(x, g):
    xf = x.astype(jnp.float32)
    y = xf * lax.rsqrt(jnp.mean(xf * xf, axis=-1, keepdims=True) + RMS_EPS)
    return (y * g.astype(jnp.float32)).astype(x.dtype)


def modulate(x, g, shift, scale):
    return rms_norm(x, g) * (1 + scale) + shift


def ada_params(cond, w, b):
    m = jax.nn.silu(cond) @ w + b
    return jnp.split(m[:, None, :], N_MOD, axis=-1)


def split_in(z):
    return jnp.split(z, [FOURIER_WIDTH, FOURIER_WIDTH + LRU_WIDTH, FOURIER_WIDTH + 2 * LRU_WIDTH], axis=-1)


def fourier_mix(u):
    b, n, _ = u.shape
    uf = u.astype(jnp.float32).reshape(b, n, FOURIER_GROUPS, FOURIER_GROUP_DIM)
    y = jnp.fft.fft2(uf, axes=(1, 3), norm="ortho").real
    return y.reshape(b, n, FOURIER_WIDTH).astype(u.dtype)


def dwconv1d(u, w, bias):
    c = u.shape[-1]
    left = LRU_CONV_W // 2
    y = lax.conv_general_dilated(u, w[:, None, :].astype(u.dtype), window_strides=(1,),
                                 padding=[(left, LRU_CONV_W - 1 - left)],
                                 dimension_numbers=("NWC", "WIO", "NWC"), feature_group_count=c)
    return y + bias


def dwconv2d(u, w, bias):
    c = u.shape[-1]
    y = lax.conv_general_dilated(u, w[:, :, None, :].astype(u.dtype), window_strides=(1, 1),
                                 padding="SAME", dimension_numbers=("NHWC", "HWIO", "NHWC"),
                                 feature_group_count=c)
    return y + bias


def lru_coeffs(u, ga_w, ga_b, gx_w, gx_b, lam):
    b, n, _ = u.shape
    uh = u.reshape(b, n, LRU_HEADS, LRU_HEAD_DIM)
    r = jax.nn.sigmoid((jnp.einsum("bnhi,hij->bnhj", uh, ga_w).reshape(b, n, LRU_WIDTH) + ga_b).astype(jnp.float32))
    i = jax.nn.sigmoid((jnp.einsum("bnhi,hij->bnhj", uh, gx_w).reshape(b, n, LRU_WIDTH) + gx_b).astype(jnp.float32))
    log_a = -LRU_C * r * jax.nn.softplus(-lam.astype(jnp.float32))
    a = jnp.exp(log_a)
    mult = jnp.sqrt(jnp.maximum(-jnp.expm1(2.0 * log_a), 1e-12))
    return a, mult * (i * u.astype(jnp.float32))


def _combine(e1, e2):
    a1, b1 = e1
    a2, b2 = e2
    return a1 * a2, a2 * b1 + b2


def linear_scan(a, b, h0):
    b = b.at[:, 0].add(a[:, 0] * h0)
    return lax.associative_scan(_combine, (a, b), axis=1)[1]


def lru_scans(u_x, conv_w, conv_b, ga_w, ga_b, gx_w, gx_b, lam, h0):
    u = dwconv1d(u_x, conv_w, conv_b)
    a_f, b_f = lru_coeffs(u, ga_w[0], ga_b[0], gx_w[0], gx_b[0], lam[0])
    a_b, b_b = lru_coeffs(u, ga_w[1], ga_b[1], gx_w[1], gx_b[1], lam[1])
    h_f = linear_scan(a_f, b_f, h0[0])
    h_b = jnp.flip(linear_scan(jnp.flip(a_b, 1), jnp.flip(b_b, 1), h0[1]), 1)
    return h_f, h_b


def merge_mixers(z_f, z_y, z_g, lru_h, w_f, w_r, w_o):
    f = fourier_mix(z_f) @ w_f
    r = (lru_h.astype(z_y.dtype) * jax.nn.gelu(z_y)) @ w_r
    g_f, g_r = jnp.split(jax.nn.sigmoid(z_g), 2, axis=-1)
    return (g_f * f + g_r * r) @ w_o


def conv_ffn(h, rows, w_up, conv_w, conv_b, w_down):
    b, n, _ = h.shape
    u = (h @ w_up).reshape(b, rows, n // rows, 2 * D_FF)
    u = dwconv2d(u, conv_w, conv_b).reshape(b, n, 2 * D_FF)
    gate, val = jnp.split(u, 2, axis=-1)
    return (jax.nn.gelu(gate) * val) @ w_down


def setup_inputs(seed: int = 0) -> dict:
    key = jax.random.key(seed)
    ks = jax.random.split(key, 24)
    f32 = jnp.float32
    L, D, R, F = DEPTH, D_MODEL, LRU_WIDTH, FOURIER_WIDTH

    def nrm(k, shape, scale):
        return jax.random.normal(k, shape, f32) * scale

    a_target = jax.random.uniform(ks[12], (L, N_DIR, R), f32, 0.9, 0.999)
    sig = a_target ** (1.0 / LRU_C)
    lru_lambda = jnp.log(sig) - jnp.log1p(-sig)
    return {
        "x": nrm(ks[0], (BATCH, SEQ, D), 1.0),
        "c": nrm(ks[1], (BATCH, D), 1.0),
        "ctx": nrm(ks[2], (BATCH, CTX_LEN, D), 1.0),
        "c_ctx": nrm(ks[3], (D,), 1.0),
        "mod_w": nrm(ks[4], (L, D, N_MOD * D), 0.5 * D ** -0.5),
        "mod_b": nrm(ks[5], (L, N_MOD * D), 0.02),
        "norm1_g": 1.0 + nrm(ks[6], (L, D), 0.02),
        "norm2_g": 1.0 + nrm(ks[7], (L, D), 0.02),
        "w_in": nrm(ks[8], (L, D, IN_WIDTH), D ** -0.5),
        "lru_conv_w": nrm(ks[9], (L, LRU_CONV_W, R), LRU_CONV_W ** -0.5),
        "lru_conv_b": nrm(ks[10], (L, R), 0.02),
        "lru_ga_w": nrm(ks[11], (L, N_DIR, LRU_HEADS, LRU_HEAD_DIM, LRU_HEAD_DIM), LRU_HEAD_DIM ** -0.5),
        "lru_ga_b": nrm(ks[13], (L, N_DIR, R), 0.02),
        "lru_gx_w": nrm(ks[14], (L, N_DIR, LRU_HEADS, LRU_HEAD_DIM, LRU_HEAD_DIM), LRU_HEAD_DIM ** -0.5),
        "lru_gx_b": nrm(ks[15], (L, N_DIR, R), 0.02),
        "lru_lambda": lru_lambda,
        "w_fourier": nrm(ks[16], (L, F, D), F ** -0.5),
        "w_lru_out": nrm(ks[17], (L, R, D), R ** -0.5),
        "w_o": nrm(ks[18], (L, D, D), D ** -0.5),
        "ffn_w_up": nrm(ks[19], (L, D, 2 * D_FF), D ** -0.5),
        "ffn_conv_w": nrm(ks[20], (L, 3, 3, 2 * D_FF), 1.0 / 3.0),
        "ffn_conv_b": nrm(ks[21], (L, 2 * D_FF), 0.02),
        "ffn_w_down": nrm(ks[22], (L, D_FF, D), D_FF ** -0.5),
        "final_g": 1.0 + nrm(ks[23], (D,), 0.02),
    }


def reference(x, c, ctx, c_ctx, mod_w, mod_b, norm1_g, norm2_g, w_in, lru_conv_w, lru_conv_b,
              lru_ga_w, lru_ga_b, lru_gx_w, lru_gx_b, lru_lambda, w_fourier, w_lru_out, w_o,
              ffn_w_up, ffn_conv_w, ffn_conv_b, ffn_w_down, final_g):
    bsz, n_lat, _ = x.shape
    rows = n_lat // GRID_W
    zero_state = jnp.zeros((N_DIR, bsz, LRU_WIDTH), jnp.float32)
    for l in range(DEPTH):
        last = l == DEPTH - 1
        sh1, sc1, g1, sh2, sc2, g2 = ada_params(c, mod_w[l], mod_b[l])
        csh1, csc1, cg1, csh2, csc2, cg2 = ada_params(c_ctx[None], mod_w[l], mod_b[l])
        lru_p = (lru_conv_w[l], lru_conv_b[l], lru_ga_w[l], lru_ga_b[l], lru_gx_w[l], lru_gx_b[l], lru_lambda[l])

        h_ctx = modulate(ctx, norm1_g[l], csh1, csc1)
        if last:
            zc_x = h_ctx @ w_in[l][:, FOURIER_WIDTH:FOURIER_WIDTH + LRU_WIDTH]
        else:
            zc_f, zc_x, zc_y, zc_g = split_in(h_ctx @ w_in[l])
        hcf, hcb = lru_scans(zc_x, *lru_p, zero_state)
        h0_lat = jnp.stack([hcf[:, -1], hcb[:, 0]])

        h_lat = modulate(x, norm1_g[l], sh1, sc1)
        z_f, z_x, z_y, z_g = split_in(h_lat @ w_in[l])
        hf, hb = lru_scans(z_x, *lru_p, h0_lat)
        x = x + g1 * merge_mixers(z_f, z_y, z_g, hf + hb, w_fourier[l], w_lru_out[l], w_o[l])
        x = x + g2 * conv_ffn(modulate(x, norm2_g[l], sh2, sc2), rows,
                              ffn_w_up[l], ffn_conv_w[l], ffn_conv_b[l], ffn_w_down[l])

        if not last:
            ctx = ctx + cg1 * merge_mixers(zc_f, zc_y, zc_g, hcf + hcb, w_fourier[l], w_lru_out[l], w_o[l])
            ctx = ctx + cg2 * conv_ffn(modulate(ctx, norm2_g[l], csh2, csc2), 1,
                                       ffn_w_up[l], ffn_conv_w[l], ffn_conv_b[l], ffn_w_down[l])
    return rms_norm(x, final_g)
```

```python
import functools
import math

import numpy as np
import jax
import jax.numpy as jnp
from jax import lax
from jax.experimental import pallas as pl
from jax.experimental.pallas import tpu as pltpu

F32 = jnp.float32
BF16 = jnp.bfloat16

GRID_W = 64
FOURIER_GROUPS = 4
LRU_CONV_W = 4
LRU_C = 8.0
N_MOD = 6
RMS_EPS = 1e-6
DFT_RADIX = 8

V7X_VMEM_BYTES = 64 * 1024 * 1024
SUBLANES = 8
LANES = 128


def _params(semantics, vmem_bytes):
    assert vmem_bytes < V7X_VMEM_BYTES, vmem_bytes
    return pltpu.CompilerParams(dimension_semantics=semantics, vmem_limit_bytes=int(vmem_bytes))


def _nbytes(shape, dtype):
    return math.prod(shape) * jnp.dtype(dtype).itemsize


def _sigmoid(x):
    return 0.5 * jnp.tanh(0.5 * x) + 0.5


def _gelu_tanh(x):
    c = math.sqrt(2.0 / math.pi)
    return 0.5 * x * (1.0 + jnp.tanh(c * (x + 0.044715 * (x * x * x))))


def _split_bf16(x):
    hi = x.astype(BF16)
    lo = (x - hi.astype(F32)).astype(BF16)
    return hi, lo


def _dot3(a, b):
    a_hi, a_lo = _split_bf16(a)
    b_hi, b_lo = _split_bf16(b)
    d = functools.partial(jnp.dot, preferred_element_type=F32)
    return d(a_hi, b_hi) + (d(a_hi, b_lo) + d(a_lo, b_hi))


def _rms_modulate(x, g, shift, scale):
    ms = jnp.mean(x * x, axis=-1, keepdims=True)
    y = x * lax.rsqrt(ms + RMS_EPS) * g
    return y * (1.0 + scale) + shift


def _mod_kernel(c_ref, w_ref, b_ref, o_ref):
    c = c_ref[...]
    s = c * _sigmoid(c)
    o_ref[...] = _dot3(s, w_ref[...]) + b_ref[...]


def _ada_params(cond, w, b):
    n, d = cond.shape
    width = w.shape[1]
    tn = d
    return pl.pallas_call(
        _mod_kernel,
        out_shape=jax.ShapeDtypeStruct((n, width), F32),
        grid=(width // tn,),
        in_specs=[
            pl.BlockSpec((n, d), lambda j: (0, 0)),
            pl.BlockSpec((d, tn), lambda j: (0, j)),
            pl.BlockSpec((1, tn), lambda j: (0, j)),
        ],
        out_specs=pl.BlockSpec((n, tn), lambda j: (0, j)),
        compiler_params=_params(("parallel",), 6 * _nbytes((d, tn), F32)),
        name="ada_params",
    )(cond, w, b.reshape(1, width))


def _inproj_kernel(x_ref, sh_ref, sc_ref, g_ref, *refs):
    n_out = len(refs) // 2
    h = _rms_modulate(x_ref[0], g_ref[...], sh_ref[0], sc_ref[0]).astype(BF16)
    for w_ref, o_ref in zip(refs[:n_out], refs[n_out:]):
        o_ref[0] = jnp.dot(h, w_ref[...], preferred_element_type=F32).astype(o_ref.dtype)


def _in_projection(x, mods, mod_row, norm_g, weights, out_dtypes, ts):
    bsz, seq, d = x.shape
    outs, out_specs = [], []
    for w, dt in zip(weights, out_dtypes):
        c = w.shape[1]
        outs.append(jax.ShapeDtypeStruct((bsz, seq, c), dt))
        out_specs.append(pl.BlockSpec((1, ts, c), lambda b, i: (b, i, 0)))
    w_specs = [pl.BlockSpec(w.shape, lambda b, i: (0, 0)) for w in weights]
    vmem = 2 * (_nbytes((ts, d), F32) + sum(_nbytes(w.shape, w.dtype) for w in weights)
                + sum(_nbytes((ts, w.shape[1]), F32) for w in weights)) + 4 * _nbytes((ts, d), F32)
    return pl.pallas_call(
        _inproj_kernel,
        out_shape=outs,
        grid=(bsz, seq // ts),
        in_specs=[
            pl.BlockSpec((1, ts, d), lambda b, i: (b, i, 0)),
            pl.BlockSpec((1, 1, d), lambda b, i: (mod_row(b), 0, 0)),
            pl.BlockSpec((1, 1, d), lambda b, i: (mod_row(b), 0, 1)),
            pl.BlockSpec((1, d), lambda b, i: (0, 0)),
        ] + w_specs,
        out_specs=out_specs,
        compiler_params=_params(("parallel", "parallel"), vmem),
        name="in_projection",
    )(x, mods, mods, norm_g.reshape(1, d), *weights)


def _lru_kernel(*refs, reverse, natural_out, steps):
    if natural_out:
        (zp_ref, zm_ref, zn_ref, cw_ref, cb_ref, wg_ref, gab_ref, gxb_ref, lam_ref, h0_ref, hprev_ref,
         o_ref, hl_ref, zpad, a_s, b_s, hbuf) = refs
    else:
        (zp_ref, zm_ref, zn_ref, cw_ref, cb_ref, wg_ref, gab_ref, gxb_ref, lam_ref, h0_ref,
         o_ref, hl_ref, zpad, a_s, b_s) = refs
    i = pl.program_id(0)
    n = pl.num_programs(0)
    ci = n - 1 - i if reverse else i
    nb = zm_ref.shape[0]
    heads, hd = wg_ref.shape[0], wg_ref.shape[1]
    rows = steps * nb
    halo = SUBLANES * nb

    @pl.when(i == 0)
    def _():
        hl_ref[...] = h0_ref[...]

    for h in range(heads):
        ln = slice(h * hd, (h + 1) * hd)
        for b in range(nb):
            zpad[h, pl.ds(b, SUBLANES, stride=nb), :] = jnp.where(ci > 0, zp_ref[b, :, ln], 0.0)
            zpad[h, pl.ds(halo + b, steps, stride=nb), :] = zm_ref[b, :, ln]
            zpad[h, pl.ds(halo + rows + b, SUBLANES, stride=nb), :] = jnp.where(ci < n - 1, zn_ref[b, :, ln], 0.0)

    left = LRU_CONV_W // 2

    def coeffs(h, carry):
        u = cb_ref[h]
        for k in range(LRU_CONV_W):
            u = u + cw_ref[h, k:k + 1, :] * zpad[h, pl.ds(halo + (k - left) * nb, rows), :]
        g = jnp.dot(u.astype(BF16), wg_ref[h], preferred_element_type=F32)
        r = _sigmoid(g[:, :hd] + gab_ref[h])
        gi = _sigmoid(g[:, hd:] + gxb_ref[h])
        nl = -lam_ref[h]
        softplus = jnp.maximum(nl, 0.0) + jnp.log1p(jnp.exp(-jnp.abs(nl)))
        log_a = r * (-LRU_C * softplus)
        a = jnp.exp(log_a)
        one_m_a2 = -jnp.tanh(log_a) * (a * a + 1.0)
        mult = jnp.sqrt(jnp.maximum(one_m_a2, 1e-12))
        a_s[h] = a
        b_s[h] = mult * (gi * u)
        return carry

    lax.fori_loop(0, heads, coeffs, 0)

    def step(t, hs):
        tt = steps - 1 - t if reverse else t
        r0 = pl.multiple_of(tt * nb, nb)
        new = []
        for h in range(heads):
            v = a_s[h, pl.ds(r0, nb), :] * hs[h] + b_s[h, pl.ds(r0, nb), :]
            if natural_out:
                hbuf[h, pl.ds(r0, nb), :] = v + hprev_ref[h, pl.ds(r0, nb), :]
            else:
                o_ref[h, pl.ds(r0, nb), :] = v
            new.append(v)
        return tuple(new)

    hs = lax.fori_loop(0, steps, step, tuple(hl_ref[h] for h in range(heads)), unroll=8)
    for h in range(heads):
        hl_ref[h] = hs[h]

    if natural_out:
        for h in range(heads):
            for b in range(nb):
                o_ref[b, :, h * hd:(h + 1) * hd] = hbuf[h, pl.ds(b, steps, stride=nb), :]


def _lru_pass(zx, lru_p, h0, h_prev, *, reverse, steps):
    conv_w, conv_b, wg, ga_b, gx_b, lam = lru_p
    nb, seq, r = zx.shape
    heads, hd = wg.shape[0], wg.shape[1]
    assert nb == SUBLANES and hd == LANES and seq % steps == 0 and steps % SUBLANES == 0
    rows = steps * nb
    n = seq // steps
    natural_out = h_prev is not None
    chunk = (lambda i: n - 1 - i) if reverse else (lambda i: i)
    per_head = lambda a: a.reshape(-1, heads, hd).transpose(1, 0, 2)
    full = lambda shape: pl.BlockSpec(shape, lambda i: (0,) * len(shape))
    tpb = steps // SUBLANES
    in_specs = [
        pl.BlockSpec((nb, SUBLANES, r), lambda i: (0, jnp.maximum(chunk(i) * tpb - 1, 0), 0)),
        pl.BlockSpec((nb, steps, r), lambda i: (0, chunk(i), 0)),
        pl.BlockSpec((nb, SUBLANES, r), lambda i: (0, jnp.minimum((chunk(i) + 1) * tpb, seq // SUBLANES - 1), 0)),
        full((heads, LRU_CONV_W, hd)), full((heads, 1, hd)), full(wg.shape), full((heads, 1, hd)),
        full((heads, 1, hd)), full((heads, 1, hd)), full((heads, nb, hd)),
    ]
    args = [zx, zx, zx, per_head(conv_w), per_head(conv_b), wg, per_head(ga_b), per_head(gx_b), per_head(lam), h0]
    slab = pl.BlockSpec((heads, rows, hd), lambda i: (0, chunk(i), 0))
    scratch = [pltpu.VMEM((heads, rows + 2 * SUBLANES * nb, hd), F32), pltpu.VMEM((heads, rows, hd), F32),
               pltpu.VMEM((heads, rows, hd), F32)]
    if natural_out:
        in_specs.append(slab)
        args.append(h_prev)
        out0 = jax.ShapeDtypeStruct((nb, seq, r), F32)
        out_spec0 = pl.BlockSpec((nb, steps, r), lambda i: (0, chunk(i), 0))
        scratch.append(pltpu.VMEM((heads, rows, hd), F32))
    else:
        out0 = jax.ShapeDtypeStruct((heads, seq * nb, hd), F32)
        out_spec0 = slab
    blk = _nbytes((rows, r), F32)
    vmem = 12 * blk + 2 * _nbytes(wg.shape, wg.dtype)
    return pl.pallas_call(
        functools.partial(_lru_kernel, reverse=reverse, natural_out=natural_out, steps=steps),
        out_shape=[out0, jax.ShapeDtypeStruct((heads, nb, hd), F32)],
        grid=(n,),
        in_specs=in_specs,
        out_specs=[out_spec0, full((heads, nb, hd))],
        scratch_shapes=scratch,
        compiler_params=_params(("arbitrary",), vmem),
        name="lru_reverse" if reverse else "lru_forward",
    )(*args)


def _cmul(re, im, c, s):
    return re * c + im * s, im * c - re * s


def _fft_lists(re, im):
    n = len(re)
    if n == 1:
        return re, im
    er, ei = _fft_lists(re[0::2], im[0::2])
    qr, qi = _fft_lists(re[1::2], im[1::2])
    out_r, out_i = [None] * n, [None] * n
    for k in range(n // 2):
        ang = 2.0 * math.pi * k / n
        if k == 0:
            tr, ti = qr[k], qi[k]
        elif 4 * k == n:
            tr, ti = qi[k], -qr[k]
        else:
            tr, ti = _cmul(qr[k], qi[k], math.cos(ang), math.sin(ang))
        out_r[k], out_i[k] = er[k] + tr, ei[k] + ti
        out_r[k + n // 2], out_i[k + n // 2] = er[k] - tr, ei[k] - ti
    return out_r, out_i


def _dft_kernel(x_ref, w_ref, twc_ref, tws_ref, u_ref, v_ref, a_s):
    rdx = DFT_RADIX
    seq = x_ref.shape[1]
    m = seq // rdx
    w = w_ref[...].astype(BF16)
    for s1 in range(0, rdx, 2):
        xa = x_ref[0, pl.ds(s1, m, stride=rdx), :]
        xb = x_ref[0, pl.ds(s1 + 1, m, stride=rdx), :]
        xx = jnp.concatenate([xa, xb], axis=1).astype(BF16)
        a = jnp.dot(w, xx, preferred_element_type=F32)
        a_s[s1] = a[:, :LANES]
        a_s[s1 + 1] = a[:, LANES:]

    blk = 128
    def rows(t, carry):
        r0 = pl.multiple_of(t * blk, blk)
        re, im = [], []
        for s1 in range(rdx):
            ar = a_s[s1, pl.ds(r0, blk), :]
            ai = a_s[s1, pl.ds(m + r0, blk), :]
            if s1:
                ar, ai = _cmul(ar, ai, twc_ref[s1, pl.ds(r0, blk), :], tws_ref[s1, pl.ds(r0, blk), :])
            re.append(ar)
            im.append(ai)
        yr, yi = _fft_lists(re, im)
        for k1 in range(rdx):
            u_ref[0, pl.ds(k1 * m + r0, blk), :] = yr[k1]
            v_ref[0, pl.ds(k1 * m + r0, blk), :] = -yi[k1]
        return carry

    lax.fori_loop(0, m // blk, rows, 0)


def _position_dft(x):
    bsz, seq, c = x.shape
    rdx = DFT_RADIX
    m = seq // rdx
    k = np.arange(m)
    ang = 2.0 * np.pi * np.outer(k, k) / m
    w = jnp.asarray(np.concatenate([np.cos(ang), -np.sin(ang)], axis=0), F32)
    tw = 2.0 * np.pi * np.outer(np.arange(rdx), k) / seq
    bc = lambda t: jnp.asarray(np.broadcast_to(t[:, :, None], (rdx, m, LANES)), F32)
    tok = pl.BlockSpec((1, seq, LANES), lambda b, j: (b, 0, j))
    vmem = 2 * (3 * _nbytes((seq, LANES), F32) + _nbytes((2 * m, m), F32) + 2 * _nbytes((rdx, m, LANES), F32)) \
        + _nbytes((rdx, 2 * m, LANES), F32) + 16 * _nbytes((2 * m, LANES), F32)
    return pl.pallas_call(
        _dft_kernel,
        out_shape=[jax.ShapeDtypeStruct((bsz, seq, c), F32)] * 2,
        grid=(bsz, c // LANES),
        in_specs=[
            tok,
            pl.BlockSpec((2 * m, m), lambda b, j: (0, 0)),
            pl.BlockSpec((rdx, m, LANES), lambda b, j: (0, 0, 0)),
            pl.BlockSpec((rdx, m, LANES), lambda b, j: (0, 0, 0)),
        ],
        out_specs=[tok, tok],
        scratch_shapes=[pltpu.VMEM((rdx, 2 * m, LANES), F32)],
        compiler_params=_params(("parallel", "parallel"), vmem),
        name="position_dft",
    )(x, w, bc(np.cos(tw)), bc(np.sin(tw)))


def _fold_kernel(cc_ref, sc_ref, w_ref, o_ref):
    w = w_ref[...]
    o_ref[0] = _dot3(cc_ref[...], w).astype(o_ref.dtype)
    o_ref[1] = (-_dot3(sc_ref[...], w)).astype(o_ref.dtype)


def _fold_channel_dft(w_fourier, seq):
    f, d = w_fourier.shape
    gd = f // FOURIER_GROUPS
    k = np.arange(gd)
    ang = 2.0 * np.pi * np.outer(k, k) / gd
    scale = 1.0 / math.sqrt(seq * gd)
    out = pl.pallas_call(
        _fold_kernel,
        out_shape=jax.ShapeDtypeStruct((2, f, d), BF16),
        grid=(FOURIER_GROUPS,),
        in_specs=[
            pl.BlockSpec((gd, gd), lambda g: (0, 0)),
            pl.BlockSpec((gd, gd), lambda g: (0, 0)),
            pl.BlockSpec((gd, d), lambda g: (g, 0)),
        ],
        out_specs=pl.BlockSpec((2, gd, d), lambda g: (0, g, 0)),
        compiler_params=_params(("parallel",), 32 * _nbytes((gd, d), F32)),
        name="fold_channel_dft",
    )(jnp.asarray(np.cos(ang) * scale, F32), jnp.asarray(np.sin(ang) * scale, F32), w_fourier)
    return out.reshape(2 * f, d)


def _merge_kernel(u_ref, v_ref, hs_ref, zy_ref, gf_ref, gr_ref, x_ref, g1_ref, sh2_ref, sc2_ref, n2_ref,
                  wcs_ref, wr_ref, wo_ref, x1_ref, h2_ref):
    uv = jnp.concatenate([u_ref[0].astype(BF16), v_ref[0].astype(BF16)], axis=1)
    f = jnp.dot(uv, wcs_ref[...], preferred_element_type=F32)
    p = (hs_ref[0] * _gelu_tanh(zy_ref[0].astype(F32))).astype(BF16)
    r = jnp.dot(p, wr_ref[...], preferred_element_type=F32)
    m = _sigmoid(gf_ref[0].astype(F32)) * f + _sigmoid(gr_ref[0].astype(F32)) * r
    o = jnp.dot(m.astype(BF16), wo_ref[...], preferred_element_type=F32)
    x1 = x_ref[0] + g1_ref[0] * o
    x1_ref[0] = x1
    h2_ref[0] = _rms_modulate(x1, n2_ref[...], sh2_ref[0], sc2_ref[0]).astype(h2_ref.dtype)


def _merge(u, v, hsum, zy, zg, x, mods, norm2_g, wcs, w_r, w_o, ts):
    bsz, seq, d = x.shape
    tok = lambda c, k=0: pl.BlockSpec((1, ts, c), lambda b, i: (b, i, k))
    mod = lambda k: pl.BlockSpec((1, 1, d), lambda b, i: (b, 0, k))
    full = lambda a: pl.BlockSpec(a.shape, lambda b, i: (0, 0))
    vmem = 2 * sum(_nbytes(a.shape, a.dtype) for a in (wcs, w_r, w_o)) + 24 * _nbytes((ts, d), F32)
    return pl.pallas_call(
        _merge_kernel,
        out_shape=[jax.ShapeDtypeStruct((bsz, seq, d), F32), jax.ShapeDtypeStruct((bsz, seq, d), BF16)],
        grid=(bsz, seq // ts),
        in_specs=[
            tok(u.shape[2]), tok(v.shape[2]), tok(hsum.shape[2]), tok(zy.shape[2]), tok(d, 0), tok(d, 1),
            tok(d), mod(2), mod(3), mod(4),
            pl.BlockSpec((1, d), lambda b, i: (0, 0)),
            full(wcs), full(w_r), full(w_o),
        ],
        out_specs=[tok(d), tok(d)],
        compiler_params=_params(("parallel", "parallel"), vmem),
        name="merge_mixers",
    )(u, v, hsum, zy, zg, zg, x, mods, mods, mods, norm2_g.reshape(1, d), wcs, w_r, w_o)


def _ffn_kernel(hp_ref, hm_ref, hn_ref, w0_ref, wn_ref, cw_ref, cb_ref, wdp_ref, wdc_ref,
                o_ref, u_s, p_s, *, row_block):
    part = pl.program_id(1)
    nparts = pl.num_programs(1)
    j = pl.program_id(2)
    nch = pl.num_programs(2)
    rows = hm_ref.shape[1]
    nslab = u_s.shape[1]
    pairs = nslab // 2
    nblk = rows // row_block
    pad = SUBLANES + GRID_W
    cur = j % 2
    nxt = 1 - cur
    d_half = o_ref.shape[2] // pairs

    def w_pair(w_ref, k):
        return w_ref[:, 2 * k * LANES:2 * (k + 1) * LANES]

    def up_main(slot, w_ref, k, r0):
        u = jnp.dot(hm_ref[0, pl.ds(r0, row_block), :], w_pair(w_ref, k), preferred_element_type=F32)
        u_s[slot, 2 * k, pl.ds(pad + r0, row_block), :] = u[:, :LANES]
        u_s[slot, 2 * k + 1, pl.ds(pad + r0, row_block), :] = u[:, LANES:]

    def up_halo(slot, w_ref):
        for k in range(pairs):
            top = jnp.where(part > 0, jnp.dot(hp_ref[0], w_pair(w_ref, k), preferred_element_type=F32), 0.0)
            bot = jnp.where(part < nparts - 1,
                            jnp.dot(hn_ref[0], w_pair(w_ref, k), preferred_element_type=F32), 0.0)
            for gv in range(2):
                u_s[slot, 2 * k + gv, SUBLANES:pad, :] = top[:, gv * LANES:(gv + 1) * LANES]
                u_s[slot, 2 * k + gv, pad + rows:pad + rows + GRID_W, :] = bot[:, gv * LANES:(gv + 1) * LANES]

    def down(slot, w_ref, k, r0):
        dlt = jnp.dot(p_s[slot, pl.ds(r0, row_block), :], w_ref[:, k * d_half:(k + 1) * d_half],
                      preferred_element_type=F32)
        o_ref[0, pl.ds(r0, row_block), k * d_half:(k + 1) * d_half] += dlt

    def blocks(fn):
        def body(t, carry):
            fn(pl.multiple_of(t * row_block, row_block))
            return carry
        lax.fori_loop(0, nblk, body, 0)

    @pl.when(j == 0)
    def _():
        zeros = jnp.zeros((SUBLANES, LANES), F32)
        for slot in range(2):
            for s in range(nslab):
                u_s[slot, s, 0:SUBLANES, :] = zeros
                u_s[slot, s, pad + rows + GRID_W:pad + rows + GRID_W + SUBLANES, :] = zeros
        p_s[1] = jnp.zeros(p_s.shape[1:], p_s.dtype)
        o_ref[...] = jnp.zeros(o_ref.shape, o_ref.dtype)
        up_halo(0, w0_ref)
        for k in range(pairs):
            blocks(functools.partial(up_main, 0, w0_ref, k))

    up_halo(nxt, wn_ref)

    sub = lax.broadcasted_iota(jnp.int32, (SUBLANES, LANES), 0)
    tiles_per_row = GRID_W // SUBLANES

    def tap_weights(wrow, dw):
        plain = jnp.broadcast_to(wrow, (SUBLANES, LANES))
        if dw == 1:
            return [plain] * tiles_per_row
        if dw == 0:
            return [jnp.where(sub == 0, 0.0, plain)] + [plain] * (tiles_per_row - 1)
        return [plain] * (tiles_per_row - 1) + [jnp.where(sub == SUBLANES - 1, 0.0, plain)]

    for k in range(pairs):
        wt = [[[tap_weights(cw_ref[3 * dh + dw:3 * dh + dw + 1, (2 * k + gv) * LANES:(2 * k + gv + 1) * LANES], dw)
                for dw in range(3)] for dh in range(3)] for gv in range(2)]
        bias = [jnp.broadcast_to(cb_ref[:, (2 * k + gv) * LANES:(2 * k + gv + 1) * LANES], (SUBLANES, LANES))
                for gv in range(2)]

        def fused(r0, k=k, wt=wt, bias=bias):
            down(nxt, wdp_ref, k, r0)
            for g in range(row_block // GRID_W):
                gated = []
                for tile in range(tiles_per_row):
                    base = pad + r0 + g * GRID_W + tile * SUBLANES
                    acc = [bias[0], bias[1]]
                    for gv in range(2):
                        for dh in range(3):
                            for dw in range(3):
                                off = (dh - 1) * GRID_W + (dw - 1)
                                acc[gv] = acc[gv] + wt[gv][dh][dw][tile] * u_s[cur, 2 * k + gv,
                                                                                 pl.ds(base + off, SUBLANES), :]
                    gated.append(_gelu_tanh(acc[0]) * acc[1])
                p_s[cur, pl.ds(r0 + g * GRID_W, GRID_W), k * LANES:(k + 1) * LANES] = (
                    jnp.concatenate(gated, axis=0).astype(p_s.dtype))
            up_main(nxt, wn_ref, k, r0)

        blocks(fused)

    @pl.when(j == nch - 1)
    def _():
        for k in range(pairs):
            blocks(functools.partial(down, cur, wdc_ref, k))


def _conv_ffn(h2, w_up, conv_w, conv_b, w_down, *, ck, row_block):
    bsz, seq, d = h2.shape
    d_ff = w_down.shape[0]
    nch = d_ff // ck
    pairs = ck // LANES
    nparts = 2
    rows = seq // nparts
    gpr = rows // GRID_W

    def chunked(a):
        lead = a.shape[0]
        return a.reshape(lead, 2, nch, pairs, LANES).transpose(0, 2, 3, 1, 4).reshape(lead, 2 * d_ff)

    w_up_c = chunked(w_up)
    cw = chunked(conv_w.reshape(9, 2 * d_ff))
    cb = chunked(conv_b.reshape(1, 2 * d_ff))
    nxt = lambda j: jnp.minimum(j + 1, nch - 1)
    prv = lambda j: jnp.maximum(j - 1, 0)
    u_rows = rows + 2 * (GRID_W + SUBLANES)
    vmem = (2 * _nbytes((rows + 2 * GRID_W, d), BF16) + 2 * _nbytes((rows, d), F32)
            + 2 * _nbytes((2 * pairs, u_rows, LANES), F32) + 2 * _nbytes((rows, ck), BF16)
            + 4 * _nbytes((d, 2 * ck), BF16) + 4 * _nbytes((ck, d), BF16) + 8 * _nbytes((row_block, d), F32))
    return pl.pallas_call(
        functools.partial(_ffn_kernel, row_block=row_block),
        out_shape=jax.ShapeDtypeStruct((bsz, seq, d), F32),
        grid=(bsz, nparts, nch),
        in_specs=[
            pl.BlockSpec((1, GRID_W, d), lambda b, p, j: (b, jnp.maximum(p * gpr - 1, 0), 0)),
            pl.BlockSpec((1, rows, d), lambda b, p, j: (b, p, 0)),
            pl.BlockSpec((1, GRID_W, d), lambda b, p, j: (b, jnp.minimum((p + 1) * gpr, seq // GRID_W - 1), 0)),
            pl.BlockSpec((d, 2 * ck), lambda b, p, j: (0, 0)),
            pl.BlockSpec((d, 2 * ck), lambda b, p, j: (0, nxt(j))),
            pl.BlockSpec((9, 2 * ck), lambda b, p, j: (0, j)),
            pl.BlockSpec((1, 2 * ck), lambda b, p, j: (0, j)),
            pl.BlockSpec((ck, d), lambda b, p, j: (prv(j), 0)),
            pl.BlockSpec((ck, d), lambda b, p, j: (j, 0)),
        ],
        out_specs=pl.BlockSpec((1, rows, d), lambda b, p, j: (b, p, 0)),
        scratch_shapes=[
            pltpu.VMEM((2, 2 * pairs, u_rows, LANES), F32),
            pltpu.VMEM((2, rows, ck), BF16),
        ],
        compiler_params=_params(("parallel", "parallel", "arbitrary"), vmem),
        name="conv_ffn",
    )(h2, h2, h2, w_up_c, w_up_c, cw, cb, w_down, w_down)


def _final_kernel(x1_ref, d_ref, g2_ref, fg_ref, o_ref):
    x2 = x1_ref[0] + g2_ref[0] * d_ref[0]
    ms = jnp.mean(x2 * x2, axis=-1, keepdims=True)
    o_ref[0] = x2 * lax.rsqrt(ms + RMS_EPS) * fg_ref[...]


def _final(x1, dlt, mods, final_g, ts):
    bsz, seq, d = x1.shape
    nat = pl.BlockSpec((1, ts, d), lambda b, i: (b, i, 0))
    return pl.pallas_call(
        _final_kernel,
        out_shape=jax.ShapeDtypeStruct((bsz, seq, d), F32),
        grid=(bsz, seq // ts),
        in_specs=[nat, nat, pl.BlockSpec((1, 1, d), lambda b, i: (b, 0, 5)),
                  pl.BlockSpec((1, d), lambda b, i: (0, 0))],
        out_specs=nat,
        compiler_params=_params(("parallel", "parallel"), 10 * _nbytes((ts, d), F32)),
        name="final_norm",
    )(x1, dlt, mods, final_g.reshape(1, d))


def kernel(x, c, ctx, c_ctx, mod_w, mod_b, norm1_g, norm2_g, w_in, lru_conv_w, lru_conv_b, lru_ga_w, lru_ga_b,
           lru_gx_w, lru_gx_b, lru_lambda, w_fourier, w_lru_out, w_o, ffn_w_up, ffn_conv_w, ffn_conv_b,
           ffn_w_down, final_g):
    bsz, seq, d = x.shape
    ctx_len = ctx.shape[1]
    assert mod_w.shape[0] == 1 and bsz == SUBLANES
    l = 0
    f = w_fourier.shape[1]
    r = w_lru_out.shape[1]
    heads, hd = lru_ga_w.shape[2], lru_ga_w.shape[3]

    cond = jnp.concatenate([c, c_ctx[None], jnp.zeros((SUBLANES - 1, d), F32)], axis=0)
    mods = _ada_params(cond, mod_w[l], mod_b[l]).reshape(cond.shape[0], 1, N_MOD * d)

    w_in_b = w_in[l].astype(BF16)
    w_f, w_x, w_y, w_g = (w_in_b[:, :f], w_in_b[:, f:f + r], w_in_b[:, f + r:f + 2 * r], w_in_b[:, f + 2 * r:])
    wg = jnp.concatenate([lru_ga_w[l], lru_gx_w[l]], axis=-1).astype(BF16)
    lru_p = lambda dr: (lru_conv_w[l], lru_conv_b[l][None], wg[dr], lru_ga_b[l, dr][None], lru_gx_b[l, dr][None],
                        lru_lambda[l, dr][None])

    (zc_x,) = _in_projection(ctx, mods, lambda b: bsz, norm1_g[l], [w_x], [F32], ts=ctx_len)
    zero = jnp.zeros((heads, bsz, hd), F32)
    _, h0_f = _lru_pass(zc_x, lru_p(0), zero, None, reverse=False, steps=64)
    _, h0_b = _lru_pass(zc_x, lru_p(1), zero, None, reverse=True, steps=64)

    z_f, z_x, z_y, z_g = _in_projection(x, mods, lambda b: b, norm1_g[l], [w_f, w_x, w_y, w_g],
                                        [F32, F32, BF16, BF16], ts=512)
    h_f, _ = _lru_pass(z_x, lru_p(0), h0_f, None, reverse=False, steps=64)
    h_sum, _ = _lru_pass(z_x, lru_p(1), h0_b, h_f, reverse=True, steps=64)
    u, v = _position_dft(z_f)
    wcs = _fold_channel_dft(w_fourier[l], seq)
    x1, h2 = _merge(u, v, h_sum, z_y, z_g, x, mods, norm2_g[l], wcs,
                    w_lru_out[l].astype(BF16), w_o[l].astype(BF16), ts=512)
    dlt = _conv_ffn(h2, ffn_w_up[l].astype(BF16), ffn_conv_w[l], ffn_conv_b[l], ffn_w_down[l].astype(BF16),
                    ck=256, row_block=256)
    return _final(x1, dlt, mods, final_g, ts=512)
```

```python
import functools
import math

import numpy as np
import jax
import jax.numpy as jnp
from jax import lax
from jax.experimental import pallas as pl
from jax.experimental.pallas import tpu as pltpu

F32 = jnp.float32
BF16 = jnp.bfloat16

GRID_W = 64
FOURIER_GROUPS = 4
LRU_CONV_W = 4
LRU_C = 8.0
N_MOD = 6
RMS_EPS = 1e-6
DFT_RADIX = 8

V7X_VMEM_BYTES = 64 * 1024 * 1024
SUBLANES = 8
LANES = 128
MXU_ROWS = 1024


def _params(semantics, vmem_bytes):
    assert vmem_bytes < V7X_VMEM_BYTES, vmem_bytes
    return pltpu.CompilerParams(dimension_semantics=semantics, vmem_limit_bytes=int(vmem_bytes))


def _nbytes(shape, dtype):
    return math.prod(shape) * jnp.dtype(dtype).itemsize


def _sigmoid(x):
    return 0.5 * jnp.tanh(0.5 * x) + 0.5


def _gelu_tanh(x):
    c = math.sqrt(2.0 / math.pi)
    return 0.5 * x * (1.0 + jnp.tanh(c * (x + 0.044715 * (x * x * x))))


def _split_bf16(x):
    hi = x.astype(BF16)
    lo = (x - hi.astype(F32)).astype(BF16)
    return hi, lo


def _dot3(a, b):
    a_hi, a_lo = _split_bf16(a)
    b_hi, b_lo = _split_bf16(b)
    d = functools.partial(jnp.dot, preferred_element_type=F32)
    return d(a_hi, b_hi) + (d(a_hi, b_lo) + d(a_lo, b_hi))


def _rms_modulate(x, g, shift, scale):
    ms = jnp.mean(x * x, axis=-1, keepdims=True)
    y = x * lax.rsqrt(ms + RMS_EPS) * g
    return y * (1.0 + scale) + shift


def _mod_kernel(c_ref, w_ref, b_ref, o_ref):
    c = c_ref[...]
    s = c * _sigmoid(c)
    o_ref[...] = _dot3(s, w_ref[...]) + b_ref[...]


def _ada_params(cond, w, b):
    n, d = cond.shape
    width = w.shape[1]
    tn = d
    return pl.pallas_call(
        _mod_kernel,
        out_shape=jax.ShapeDtypeStruct((n, width), F32),
        grid=(width // tn,),
        in_specs=[
            pl.BlockSpec((n, d), lambda j: (0, 0)),
            pl.BlockSpec((d, tn), lambda j: (0, j)),
            pl.BlockSpec((1, tn), lambda j: (0, j)),
        ],
        out_specs=pl.BlockSpec((n, tn), lambda j: (0, j)),
        compiler_params=_params(("parallel",), 6 * _nbytes((d, tn), F32)),
        name="ada_params",
    )(cond, w, b.reshape(1, width))


def _inproj_kernel(x_ref, sh_ref, sc_ref, g_ref, *refs):
    n_out = len(refs) // 2
    h = _rms_modulate(x_ref[0], g_ref[...], sh_ref[0], sc_ref[0]).astype(BF16)
    for w_ref, o_ref in zip(refs[:n_out], refs[n_out:]):
        o_ref[0] = jnp.dot(h, w_ref[...], preferred_element_type=F32).astype(o_ref.dtype)


def _in_projection(x, mods, mod_row, norm_g, weights, out_dtypes, ts):
    bsz, seq, d = x.shape
    outs, out_specs = [], []
    for w, dt in zip(weights, out_dtypes):
        c = w.shape[1]
        outs.append(jax.ShapeDtypeStruct((bsz, seq, c), dt))
        out_specs.append(pl.BlockSpec((1, ts, c), lambda b, i: (b, i, 0)))
    w_specs = [pl.BlockSpec(w.shape, lambda b, i: (0, 0)) for w in weights]
    vmem = 2 * (_nbytes((ts, d), F32) + sum(_nbytes(w.shape, w.dtype) for w in weights)
                + sum(_nbytes((ts, w.shape[1]), F32) for w in weights)) + 4 * _nbytes((ts, d), F32)
    return pl.pallas_call(
        _inproj_kernel,
        out_shape=outs,
        grid=(bsz, seq // ts),
        in_specs=[
            pl.BlockSpec((1, ts, d), lambda b, i: (b, i, 0)),
            pl.BlockSpec((1, 1, d), lambda b, i: (mod_row(b), 0, 0)),
            pl.BlockSpec((1, 1, d), lambda b, i: (mod_row(b), 0, 1)),
            pl.BlockSpec((1, d), lambda b, i: (0, 0)),
        ] + w_specs,
        out_specs=out_specs,
        compiler_params=_params(("parallel", "parallel"), vmem),
        name="in_projection",
    )(x, mods, mods, norm_g.reshape(1, d), *weights)


def _lru_kernel(*refs, reverse, natural_out, steps):
    if natural_out:
        (zp_ref, zm_ref, zn_ref, cw_ref, cb_ref, wg_ref, gab_ref, gxb_ref, lam_ref, h0_ref, hprev_ref,
         o_ref, hl_ref, zpad, a_s, b_s, hbuf) = refs
    else:
        (zp_ref, zm_ref, zn_ref, cw_ref, cb_ref, wg_ref, gab_ref, gxb_ref, lam_ref, h0_ref,
         o_ref, hl_ref, zpad, a_s, b_s) = refs
    i = pl.program_id(0)
    n = pl.num_programs(0)
    ci = n - 1 - i if reverse else i
    nb = zm_ref.shape[0]
    heads, hd = wg_ref.shape[0], wg_ref.shape[1]
    rows = steps * nb
    halo = SUBLANES * nb

    @pl.when(i == 0)
    def _():
        hl_ref[...] = h0_ref[...]

    for h in range(heads):
        ln = slice(h * hd, (h + 1) * hd)
        for b in range(nb):
            zpad[h, pl.ds(b, SUBLANES, stride=nb), :] = jnp.where(ci > 0, zp_ref[b, :, ln], 0.0)
            zpad[h, pl.ds(halo + b, steps, stride=nb), :] = zm_ref[b, :, ln]
            zpad[h, pl.ds(halo + rows + b, SUBLANES, stride=nb), :] = jnp.where(ci < n - 1, zn_ref[b, :, ln], 0.0)

    left = LRU_CONV_W // 2

    def coeffs(h, carry):
        u = cb_ref[h]
        for k in range(LRU_CONV_W):
            u = u + cw_ref[h, k:k + 1, :] * zpad[h, pl.ds(halo + (k - left) * nb, rows), :]
        g = jnp.dot(u.astype(BF16), wg_ref[h], preferred_element_type=F32)
        r = _sigmoid(g[:, :hd] + gab_ref[h])
        gi = _sigmoid(g[:, hd:] + gxb_ref[h])
        nl = -lam_ref[h]
        softplus = jnp.maximum(nl, 0.0) + jnp.log1p(jnp.exp(-jnp.abs(nl)))
        log_a = r * (-LRU_C * softplus)
        a = jnp.exp(log_a)
        one_m_a2 = -jnp.tanh(log_a) * (a * a + 1.0)
        mult = jnp.sqrt(jnp.maximum(one_m_a2, 1e-12))
        a_s[h] = a
        b_s[h] = mult * (gi * u)
        return carry

    lax.fori_loop(0, heads, coeffs, 0)

    def step(t, hs):
        tt = steps - 1 - t if reverse else t
        r0 = pl.multiple_of(tt * nb, nb)
        new = []
        for h in range(heads):
            v = a_s[h, pl.ds(r0, nb), :] * hs[h] + b_s[h, pl.ds(r0, nb), :]
            if natural_out:
                hbuf[h, pl.ds(r0, nb), :] = v + hprev_ref[h, pl.ds(r0, nb), :]
            else:
                o_ref[h, pl.ds(r0, nb), :] = v
            new.append(v)
        return tuple(new)

    hs = lax.fori_loop(0, steps, step, tuple(hl_ref[h] for h in range(heads)), unroll=8)
    for h in range(heads):
        hl_ref[h] = hs[h]

    if natural_out:
        for h in range(heads):
            for b in range(nb):
                o_ref[b, :, h * hd:(h + 1) * hd] = hbuf[h, pl.ds(b, steps, stride=nb), :]


def _lru_pass(zx, lru_p, h0, h_prev, *, reverse, steps):
    conv_w, conv_b, wg, ga_b, gx_b, lam = lru_p
    nb, seq, r = zx.shape
    heads, hd = wg.shape[0], wg.shape[1]
    assert nb == SUBLANES and hd == LANES and seq % steps == 0 and steps % SUBLANES == 0
    rows = steps * nb
    n = seq // steps
    natural_out = h_prev is not None
    chunk = (lambda i: n - 1 - i) if reverse else (lambda i: i)
    per_head = lambda a: a.reshape(-1, heads, hd).transpose(1, 0, 2)
    full = lambda shape: pl.BlockSpec(shape, lambda i: (0,) * len(shape))
    tpb = steps // SUBLANES
    in_specs = [
        pl.BlockSpec((nb, SUBLANES, r), lambda i: (0, jnp.maximum(chunk(i) * tpb - 1, 0), 0)),
        pl.BlockSpec((nb, steps, r), lambda i: (0, chunk(i), 0)),
        pl.BlockSpec((nb, SUBLANES, r), lambda i: (0, jnp.minimum((chunk(i) + 1) * tpb, seq // SUBLANES - 1), 0)),
        full((heads, LRU_CONV_W, hd)), full((heads, 1, hd)), full(wg.shape), full((heads, 1, hd)),
        full((heads, 1, hd)), full((heads, 1, hd)), full((heads, nb, hd)),
    ]
    args = [zx, zx, zx, per_head(conv_w), per_head(conv_b), wg, per_head(ga_b), per_head(gx_b), per_head(lam), h0]
    slab = pl.BlockSpec((heads, rows, hd), lambda i: (0, chunk(i), 0))
    scratch = [pltpu.VMEM((heads, rows + 2 * SUBLANES * nb, hd), F32), pltpu.VMEM((heads, rows, hd), F32),
               pltpu.VMEM((heads, rows, hd), F32)]
    if natural_out:
        in_specs.append(slab)
        args.append(h_prev)
        out0 = jax.ShapeDtypeStruct((nb, seq, r), F32)
        out_spec0 = pl.BlockSpec((nb, steps, r), lambda i: (0, chunk(i), 0))
        scratch.append(pltpu.VMEM((heads, rows, hd), F32))
    else:
        out0 = jax.ShapeDtypeStruct((heads, seq * nb, hd), F32)
        out_spec0 = slab
    blk = _nbytes((rows, r), F32)
    vmem = 12 * blk + 2 * _nbytes(wg.shape, wg.dtype)
    return pl.pallas_call(
        functools.partial(_lru_kernel, reverse=reverse, natural_out=natural_out, steps=steps),
        out_shape=[out0, jax.ShapeDtypeStruct((heads, nb, hd), F32)],
        grid=(n,),
        in_specs=in_specs,
        out_specs=[out_spec0, full((heads, nb, hd))],
        scratch_shapes=scratch,
        compiler_params=_params(("arbitrary",), vmem),
        name="lru_reverse" if reverse else "lru_forward",
    )(*args)


def _cmul(re, im, c, s):
    return re * c + im * s, im * c - re * s


def _fft_lists(re, im):
    n = len(re)
    if n == 1:
        return re, im
    er, ei = _fft_lists(re[0::2], im[0::2])
    qr, qi = _fft_lists(re[1::2], im[1::2])
    out_r, out_i = [None] * n, [None] * n
    for k in range(n // 2):
        ang = 2.0 * math.pi * k / n
        if k == 0:
            tr, ti = qr[k], qi[k]
        elif 4 * k == n:
            tr, ti = qi[k], -qr[k]
        else:
            tr, ti = _cmul(qr[k], qi[k], math.cos(ang), math.sin(ang))
        out_r[k], out_i[k] = er[k] + tr, ei[k] + ti
        out_r[k + n // 2], out_i[k + n // 2] = er[k] - tr, ei[k] - ti
    return out_r, out_i


def _dft_kernel(x_ref, w_ref, twc_ref, tws_ref, u_ref, v_ref, a_s):
    rdx = DFT_RADIX
    seq = x_ref.shape[1]
    m = seq // rdx
    w = w_ref[...].astype(BF16)
    for s1 in range(0, rdx, 2):
        xa = x_ref[0, pl.ds(s1, m, stride=rdx), :]
        xb = x_ref[0, pl.ds(s1 + 1, m, stride=rdx), :]
        xx = jnp.concatenate([xa, xb], axis=1).astype(BF16)
        a = jnp.dot(w, xx, preferred_element_type=F32)
        a_s[s1] = a[:, :LANES]
        a_s[s1 + 1] = a[:, LANES:]

    blk = 128

    def rows(t, carry):
        r0 = pl.multiple_of(t * blk, blk)
        re, im = [], []
        for s1 in range(rdx):
            ar = a_s[s1, pl.ds(r0, blk), :]
            ai = a_s[s1, pl.ds(m + r0, blk), :]
            if s1:
                ar, ai = _cmul(ar, ai, twc_ref[s1, pl.ds(r0, blk), :], tws_ref[s1, pl.ds(r0, blk), :])
            re.append(ar)
            im.append(ai)
        yr, yi = _fft_lists(re, im)
        for k1 in range(rdx):
            u_ref[0, pl.ds(k1 * m + r0, blk), :] = yr[k1].astype(u_ref.dtype)
            v_ref[0, pl.ds(k1 * m + r0, blk), :] = (-yi[k1]).astype(v_ref.dtype)
        return carry

    lax.fori_loop(0, m // blk, rows, 0)


def _position_dft(x):
    bsz, seq, c = x.shape
    rdx = DFT_RADIX
    m = seq // rdx
    k = np.arange(m)
    ang = 2.0 * np.pi * np.outer(k, k) / m
    w = jnp.asarray(np.concatenate([np.cos(ang), -np.sin(ang)], axis=0), F32)
    tw = 2.0 * np.pi * np.outer(np.arange(rdx), k) / seq
    bc = lambda t: jnp.asarray(np.broadcast_to(t[:, :, None], (rdx, m, LANES)), F32)
    tok = pl.BlockSpec((1, seq, LANES), lambda b, j: (b, 0, j))
    vmem = 2 * (3 * _nbytes((seq, LANES), F32) + _nbytes((2 * m, m), F32) + 2 * _nbytes((rdx, m, LANES), F32)) \
        + _nbytes((rdx, 2 * m, LANES), F32) + 16 * _nbytes((2 * m, LANES), F32)
    return pl.pallas_call(
        _dft_kernel,
        out_shape=[jax.ShapeDtypeStruct((bsz, seq, c), BF16)] * 2,
        grid=(bsz, c // LANES),
        in_specs=[
            tok,
            pl.BlockSpec((2 * m, m), lambda b, j: (0, 0)),
            pl.BlockSpec((rdx, m, LANES), lambda b, j: (0, 0, 0)),
            pl.BlockSpec((rdx, m, LANES), lambda b, j: (0, 0, 0)),
        ],
        out_specs=[tok, tok],
        scratch_shapes=[pltpu.VMEM((rdx, 2 * m, LANES), F32)],
        compiler_params=_params(("parallel", "parallel"), vmem),
        name="position_dft",
    )(x, w, bc(np.cos(tw)), bc(np.sin(tw)))


def _fold_kernel(cc_ref, sc_ref, w_ref, o_ref):
    w = w_ref[...]
    o_ref[0] = _dot3(cc_ref[...], w).astype(o_ref.dtype)
    o_ref[1] = (-_dot3(sc_ref[...], w)).astype(o_ref.dtype)


def _fold_channel_dft(w_fourier, seq):
    f, d = w_fourier.shape
    gd = f // FOURIER_GROUPS
    k = np.arange(gd)
    ang = 2.0 * np.pi * np.outer(k, k) / gd
    scale = 1.0 / math.sqrt(seq * gd)
    out = pl.pallas_call(
        _fold_kernel,
        out_shape=jax.ShapeDtypeStruct((2, f, d), BF16),
        grid=(FOURIER_GROUPS,),
        in_specs=[
            pl.BlockSpec((gd, gd), lambda g: (0, 0)),
            pl.BlockSpec((gd, gd), lambda g: (0, 0)),
            pl.BlockSpec((gd, d), lambda g: (g, 0)),
        ],
        out_specs=pl.BlockSpec((2, gd, d), lambda g: (0, g, 0)),
        compiler_params=_params(("parallel",), 32 * _nbytes((gd, d), F32)),
        name="fold_channel_dft",
    )(jnp.asarray(np.cos(ang) * scale, F32), jnp.asarray(np.sin(ang) * scale, F32), w_fourier)
    return out.reshape(2 * f, d)


def _merge_kernel(u_ref, v_ref, hs_ref, zy_ref, gf_ref, gr_ref, x_ref, g1_ref, sh2_ref, sc2_ref, n2_ref,
                  wcs_ref, wr_ref, wo_ref, x1_ref, h2_ref):
    uv = jnp.concatenate([u_ref[0], v_ref[0]], axis=1)
    f = jnp.dot(uv, wcs_ref[...], preferred_element_type=F32)
    p = (hs_ref[0] * _gelu_tanh(zy_ref[0].astype(F32))).astype(BF16)
    r = jnp.dot(p, wr_ref[...], preferred_element_type=F32)
    m = _sigmoid(gf_ref[0].astype(F32)) * f + _sigmoid(gr_ref[0].astype(F32)) * r
    o = jnp.dot(m.astype(BF16), wo_ref[...], preferred_element_type=F32)
    x1 = x_ref[0] + g1_ref[0] * o
    x1_ref[0] = x1
    h2_ref[0] = _rms_modulate(x1, n2_ref[...], sh2_ref[0], sc2_ref[0]).astype(h2_ref.dtype)


def _merge(u, v, hsum, zy, zg, x, mods, norm2_g, wcs, w_r, w_o, ts):
    bsz, seq, d = x.shape
    tok = lambda c, k=0: pl.BlockSpec((1, ts, c), lambda b, i: (b, i, k))
    mod = lambda k: pl.BlockSpec((1, 1, d), lambda b, i: (b, 0, k))
    full = lambda a: pl.BlockSpec(a.shape, lambda b, i: (0, 0))
    vmem = 2 * sum(_nbytes(a.shape, a.dtype) for a in (wcs, w_r, w_o)) + 24 * _nbytes((ts, d), F32)
    return pl.pallas_call(
        _merge_kernel,
        out_shape=[jax.ShapeDtypeStruct((bsz, seq, d), F32), jax.ShapeDtypeStruct((bsz, seq, d), BF16)],
        grid=(bsz, seq // ts),
        in_specs=[
            tok(u.shape[2]), tok(v.shape[2]), tok(hsum.shape[2]), tok(zy.shape[2]), tok(d, 0), tok(d, 1),
            tok(d), mod(2), mod(3), mod(4),
            pl.BlockSpec((1, d), lambda b, i: (0, 0)),
            full(wcs), full(w_r), full(w_o),
        ],
        out_specs=[tok(d), tok(d)],
        compiler_params=_params(("parallel", "parallel"), vmem),
        name="merge_mixers",
    )(u, v, hsum, zy, zg, zg, x, mods, mods, mods, norm2_g.reshape(1, d), wcs, w_r, w_o)


def _ffn_kernel(hp_ref, hm_ref, hn_ref, wg_ref, wv_ref, cwg_ref, cwv_ref, cbg_ref, cbv_ref, wd_ref,
                x1_ref, g2_ref, fg_ref, o_ref, u_s, p_s):
    part = pl.program_id(1)
    nparts = pl.num_programs(1)
    j = pl.program_id(2)
    nch = pl.num_programs(2)
    rows, d = hm_ref.shape[1], hm_ref.shape[2]
    pairs = wd_ref.shape[0] // LANES
    pad = SUBLANES + GRID_W
    tiles_per_row = GRID_W // SUBLANES
    mm_rows = min(MXU_ROWS, rows)

    @pl.when(j == 0)
    def _():
        zeros = jnp.zeros((SUBLANES, LANES), F32)
        for s in range(2 * pairs):
            u_s[s, 0:SUBLANES, :] = zeros
            u_s[s, pad + rows + GRID_W:pad + rows + GRID_W + SUBLANES, :] = zeros
        o_ref[...] = jnp.zeros(o_ref.shape, o_ref.dtype)

    for k in range(pairs):
        ln = slice(k * LANES, (k + 1) * LANES)
        w = jnp.concatenate([wg_ref[:, ln], wv_ref[:, ln]], axis=1)
        up = lambda h, w=w: jnp.dot(h, w, preferred_element_type=F32)
        top = jnp.where(part > 0, up(hp_ref[0]), 0.0)
        bot = jnp.where(part < nparts - 1, up(hn_ref[0]), 0.0)
        for gv in range(2):
            u_s[gv * pairs + k, SUBLANES:pad, :] = top[:, gv * LANES:(gv + 1) * LANES]
            u_s[gv * pairs + k, pad + rows:pad + rows + GRID_W, :] = bot[:, gv * LANES:(gv + 1) * LANES]
        for r0 in range(0, rows, mm_rows):
            u = up(hm_ref[0, r0:r0 + mm_rows, :])
            for gv in range(2):
                u_s[gv * pairs + k, pad + r0:pad + r0 + mm_rows, :] = u[:, gv * LANES:(gv + 1) * LANES]

    sub = lax.broadcasted_iota(jnp.int32, (SUBLANES, LANES), 0)
    for k in range(pairs):
        ln = slice(k * LANES, (k + 1) * LANES)
        cw = (cwg_ref, cwv_ref)
        cb = (cbg_ref, cbv_ref)
        wt = [[[jnp.broadcast_to(cw[gv][3 * dh + dw:3 * dh + dw + 1, ln], (SUBLANES, LANES)) for dw in range(3)]
               for dh in range(3)] for gv in range(2)]
        bias = [jnp.broadcast_to(cb[gv][:, ln], (SUBLANES, LANES)) for gv in range(2)]

        def grid_row(g, carry, k=k, ln=ln, wt=wt, bias=bias):
            g0 = pl.multiple_of(g * GRID_W, GRID_W)
            gated = []
            for tile in range(tiles_per_row):
                base = pad + g0 + tile * SUBLANES
                acc = []
                for gv in range(2):
                    cols = []
                    for dw in range(3):
                        col = None
                        for dh in range(3):
                            off = (dh - 1) * GRID_W + (dw - 1)
                            term = wt[gv][dh][dw] * u_s[gv * pairs + k, pl.ds(base + off, SUBLANES), :]
                            col = term if col is None else col + term
                        cols.append(col)
                    if tile == 0:
                        cols[0] = jnp.where(sub == 0, 0.0, cols[0])
                    if tile == tiles_per_row - 1:
                        cols[2] = jnp.where(sub == SUBLANES - 1, 0.0, cols[2])
                    acc.append((bias[gv] + cols[1]) + (cols[0] + cols[2]))
                gated.append(_gelu_tanh(acc[0]) * acc[1])
            p_s[pl.ds(g0, GRID_W), ln] = jnp.concatenate(gated, axis=0).astype(p_s.dtype)
            return carry

        lax.fori_loop(0, rows // GRID_W, grid_row, 0)

    n_tile = 2 * LANES
    for r0 in range(0, rows, mm_rows):
        p = p_s[r0:r0 + mm_rows, :]
        for n0 in range(0, d, n_tile):
            o_ref[0, r0:r0 + mm_rows, n0:n0 + n_tile] += jnp.dot(p, wd_ref[:, n0:n0 + n_tile],
                                                                  preferred_element_type=F32)

    @pl.when(j == nch - 1)
    def _():
        blk = 256

        def finish(t, carry):
            r0 = pl.multiple_of(t * blk, blk)
            x2 = x1_ref[0, pl.ds(r0, blk), :] + g2_ref[0] * o_ref[0, pl.ds(r0, blk), :]
            ms = jnp.mean(x2 * x2, axis=-1, keepdims=True)
            o_ref[0, pl.ds(r0, blk), :] = x2 * lax.rsqrt(ms + RMS_EPS) * fg_ref[...]
            return carry

        lax.fori_loop(0, rows // blk, finish, 0)


def _conv_ffn_final(h2, x1, mods, final_g, w_up, conv_w, conv_b, w_down, *, ck):
    bsz, seq, d = h2.shape
    d_ff = w_down.shape[0]
    nch = d_ff // ck
    pairs = ck // LANES
    nparts = 2
    rows = seq // nparts
    gpr = rows // GRID_W
    cw = conv_w.reshape(9, 2 * d_ff)
    cb = conv_b.reshape(1, 2 * d_ff)
    u_rows = rows + 2 * (GRID_W + SUBLANES)
    gate = lambda shape: pl.BlockSpec(shape, lambda b, p, j: (0, j))
    value = lambda shape: pl.BlockSpec(shape, lambda b, p, j: (0, nch + j))
    img = pl.BlockSpec((1, rows, d), lambda b, p, j: (b, p, 0))
    vmem = (2 * _nbytes((rows + 2 * GRID_W, d), BF16) + 4 * _nbytes((rows, d), F32)
            + _nbytes((2 * pairs, u_rows, LANES), F32) + _nbytes((rows, ck), BF16)
            + 4 * _nbytes((d, ck), BF16) + 2 * _nbytes((ck, d), BF16) + 4 * _nbytes((MXU_ROWS, 2 * LANES), F32))
    return pl.pallas_call(
        _ffn_kernel,
        out_shape=jax.ShapeDtypeStruct((bsz, seq, d), F32),
        grid=(bsz, nparts, nch),
        in_specs=[
            pl.BlockSpec((1, GRID_W, d), lambda b, p, j: (b, jnp.maximum(p * gpr - 1, 0), 0)),
            img,
            pl.BlockSpec((1, GRID_W, d), lambda b, p, j: (b, jnp.minimum((p + 1) * gpr, seq // GRID_W - 1), 0)),
            gate((d, ck)), value((d, ck)), gate((9, ck)), value((9, ck)), gate((1, ck)), value((1, ck)),
            pl.BlockSpec((ck, d), lambda b, p, j: (j, 0)),
            img,
            pl.BlockSpec((1, 1, d), lambda b, p, j: (b, 0, 5)),
            pl.BlockSpec((1, d), lambda b, p, j: (0, 0)),
        ],
        out_specs=img,
        scratch_shapes=[
            pltpu.VMEM((2 * pairs, u_rows, LANES), F32),
            pltpu.VMEM((rows, ck), BF16),
        ],
        compiler_params=_params(("parallel", "parallel", "arbitrary"), vmem),
        name="conv_ffn",
    )(h2, h2, h2, w_up, w_up, cw, cw, cb, cb, w_down, x1, mods, final_g.reshape(1, d))


def kernel(x, c, ctx, c_ctx, mod_w, mod_b, norm1_g, norm2_g, w_in, lru_conv_w, lru_conv_b, lru_ga_w, lru_ga_b,
           lru_gx_w, lru_gx_b, lru_lambda, w_fourier, w_lru_out, w_o, ffn_w_up, ffn_conv_w, ffn_conv_b,
           ffn_w_down, final_g):
    bsz, seq, d = x.shape
    ctx_len = ctx.shape[1]
    assert mod_w.shape[0] == 1 and bsz == SUBLANES
    l = 0
    f = w_fourier.shape[1]
    r = w_lru_out.shape[1]
    heads, hd = lru_ga_w.shape[2], lru_ga_w.shape[3]

    cond = jnp.concatenate([c, c_ctx[None], jnp.zeros((SUBLANES - 1, d), F32)], axis=0)
    mods = _ada_params(cond, mod_w[l], mod_b[l]).reshape(cond.shape[0], 1, N_MOD * d)

    w_in_b = w_in[l].astype(BF16)
    w_f, w_x, w_y, w_g = (w_in_b[:, :f], w_in_b[:, f:f + r], w_in_b[:, f + r:f + 2 * r], w_in_b[:, f + 2 * r:])
    wg = jnp.concatenate([lru_ga_w[l], lru_gx_w[l]], axis=-1).astype(BF16)
    lru_p = lambda dr: (lru_conv_w[l], lru_conv_b[l][None], wg[dr], lru_ga_b[l, dr][None], lru_gx_b[l, dr][None],
                        lru_lambda[l, dr][None])

    (zc_x,) = _in_projection(ctx, mods, lambda b: bsz, norm1_g[l], [w_x], [F32], ts=ctx_len)
    zero = jnp.zeros((heads, bsz, hd), F32)
    _, h0_f = _lru_pass(zc_x, lru_p(0), zero, None, reverse=False, steps=64)
    _, h0_b = _lru_pass(zc_x, lru_p(1), zero, None, reverse=True, steps=64)

    z_f, z_x, z_y, z_g = _in_projection(x, mods, lambda b: b, norm1_g[l], [w_f, w_x, w_y, w_g],
                                        [F32, F32, BF16, BF16], ts=512)
    h_f, _ = _lru_pass(z_x, lru_p(0), h0_f, None, reverse=False, steps=64)
    h_sum, _ = _lru_pass(z_x, lru_p(1), h0_b, h_f, reverse=True, steps=64)
    u, v = _position_dft(z_f)
    wcs = _fold_channel_dft(w_fourier[l], seq)
    x1, h2 = _merge(u, v, h_sum, z_y, z_g, x, mods, norm2_g[l], wcs,
                    w_lru_out[l].astype(BF16), w_o[l].astype(BF16), ts=512)
    return _conv_ffn_final(h2, x1, mods, final_g, ffn_w_up[l].astype(BF16), ffn_conv_w[l], ffn_conv_b[l],
                           ffn_w_down[l].astype(BF16), ck=256)
```

```python
import functools
import math

import numpy as np
import jax
import jax.numpy as jnp
from jax import lax
from jax.experimental import pallas as pl
from jax.experimental.pallas import tpu as pltpu

F32 = jnp.float32
BF16 = jnp.bfloat16

GRID_W = 64
FOURIER_GROUPS = 4
LRU_CONV_W = 4
LRU_C = 8.0
N_MOD = 6
RMS_EPS = 1e-6
DFT_RADIX = 8

V7X_VMEM_BYTES = 64 * 1024 * 1024
SUBLANES = 8
LANES = 128
PIPE_ROWS = 512


def _params(semantics, vmem_bytes):
    assert vmem_bytes < V7X_VMEM_BYTES, vmem_bytes
    return pltpu.CompilerParams(dimension_semantics=semantics, vmem_limit_bytes=int(vmem_bytes))


def _nbytes(shape, dtype):
    return math.prod(shape) * jnp.dtype(dtype).itemsize


def _sigmoid(x):
    return 0.5 * jnp.tanh(0.5 * x) + 0.5


def _gelu_tanh(x):
    c = math.sqrt(2.0 / math.pi)
    return 0.5 * x * (1.0 + jnp.tanh(c * (x + 0.044715 * (x * x * x))))


def _split_bf16(x):
    hi = x.astype(BF16)
    lo = (x - hi.astype(F32)).astype(BF16)
    return hi, lo


def _dot3(a, b):
    a_hi, a_lo = _split_bf16(a)
    b_hi, b_lo = _split_bf16(b)
    d = functools.partial(jnp.dot, preferred_element_type=F32)
    return d(a_hi, b_hi) + (d(a_hi, b_lo) + d(a_lo, b_hi))


def _rms_modulate(x, g, shift, scale):
    ms = jnp.mean(x * x, axis=-1, keepdims=True)
    y = x * lax.rsqrt(ms + RMS_EPS) * g
    return y * (1.0 + scale) + shift


def _mod_kernel(c_ref, w_ref, b_ref, o_ref):
    c = c_ref[...]
    s = c * _sigmoid(c)
    o_ref[...] = _dot3(s, w_ref[...]) + b_ref[...]


def _ada_params(cond, w, b):
    n, d = cond.shape
    width = w.shape[1]
    tn = d
    return pl.pallas_call(
        _mod_kernel,
        out_shape=jax.ShapeDtypeStruct((n, width), F32),
        grid=(width // tn,),
        in_specs=[
            pl.BlockSpec((n, d), lambda j: (0, 0)),
            pl.BlockSpec((d, tn), lambda j: (0, j)),
            pl.BlockSpec((1, tn), lambda j: (0, j)),
        ],
        out_specs=pl.BlockSpec((n, tn), lambda j: (0, j)),
        compiler_params=_params(("parallel",), 6 * _nbytes((d, tn), F32)),
        name="ada_params",
    )(cond, w, b.reshape(1, width))


def _inproj_kernel(x_ref, sh_ref, sc_ref, g_ref, *refs):
    n_out = len(refs) // 2
    h = _rms_modulate(x_ref[0], g_ref[...], sh_ref[0], sc_ref[0]).astype(BF16)
    for w_ref, o_ref in zip(refs[:n_out], refs[n_out:]):
        o_ref[0] = jnp.dot(h, w_ref[...], preferred_element_type=F32).astype(o_ref.dtype)


def _in_projection(x, mods, mod_row, norm_g, weights, out_dtypes, ts):
    bsz, seq, d = x.shape
    outs, out_specs = [], []
    for w, dt in zip(weights, out_dtypes):
        c = w.shape[1]
        outs.append(jax.ShapeDtypeStruct((bsz, seq, c), dt))
        out_specs.append(pl.BlockSpec((1, ts, c), lambda b, i: (b, i, 0)))
    w_specs = [pl.BlockSpec(w.shape, lambda b, i: (0, 0)) for w in weights]
    vmem = 2 * (_nbytes((ts, d), F32) + sum(_nbytes(w.shape, w.dtype) for w in weights)
                + sum(_nbytes((ts, w.shape[1]), F32) for w in weights)) + 4 * _nbytes((ts, d), F32)
    return pl.pallas_call(
        _inproj_kernel,
        out_shape=outs,
        grid=(bsz, seq // ts),
        in_specs=[
            pl.BlockSpec((1, ts, d), lambda b, i: (b, i, 0)),
            pl.BlockSpec((1, 1, d), lambda b, i: (mod_row(b), 0, 0)),
            pl.BlockSpec((1, 1, d), lambda b, i: (mod_row(b), 0, 1)),
            pl.BlockSpec((1, d), lambda b, i: (0, 0)),
        ] + w_specs,
        out_specs=out_specs,
        compiler_params=_params(("parallel", "parallel"), vmem),
        name="in_projection",
    )(x, mods, mods, norm_g.reshape(1, d), *weights)


def _lru_kernel(*refs, reverse, natural_out, steps):
    if natural_out:
        (zp_ref, zm_ref, zn_ref, cw_ref, cb_ref, wg_ref, gab_ref, gxb_ref, lam_ref, h0_ref, hprev_ref,
         o_ref, hl_ref, zpad, a_s, b_s, hbuf) = refs
    else:
        (zp_ref, zm_ref, zn_ref, cw_ref, cb_ref, wg_ref, gab_ref, gxb_ref, lam_ref, h0_ref,
         o_ref, hl_ref, zpad, a_s, b_s) = refs
    i = pl.program_id(0)
    n = pl.num_programs(0)
    ci = n - 1 - i if reverse else i
    nb = zm_ref.shape[0]
    heads, hd = wg_ref.shape[0], wg_ref.shape[1]
    rows = steps * nb
    halo = SUBLANES * nb

    @pl.when(i == 0)
    def _():
        hl_ref[...] = h0_ref[...]

    for h in range(heads):
        ln = slice(h * hd, (h + 1) * hd)
        for b in range(nb):
            zpad[h, pl.ds(b, SUBLANES, stride=nb), :] = jnp.where(ci > 0, zp_ref[b, :, ln], 0.0)
            zpad[h, pl.ds(halo + b, steps, stride=nb), :] = zm_ref[b, :, ln]
            zpad[h, pl.ds(halo + rows + b, SUBLANES, stride=nb), :] = jnp.where(ci < n - 1, zn_ref[b, :, ln], 0.0)

    left = LRU_CONV_W // 2

    def coeffs(h, carry):
        u = cb_ref[h]
        for k in range(LRU_CONV_W):
            u = u + cw_ref[h, k:k + 1, :] * zpad[h, pl.ds(halo + (k - left) * nb, rows), :]
        g = jnp.dot(u.astype(BF16), wg_ref[h], preferred_element_type=F32)
        r = 0.5 * jnp.tanh(g[:, :hd] + gab_ref[h]) + 0.5
        gi = 0.5 * jnp.tanh(g[:, hd:] + gxb_ref[h]) + 0.5
        nl = -lam_ref[h]
        softplus = jnp.maximum(nl, 0.0) + jnp.log1p(jnp.exp(-jnp.abs(nl)))
        log_a = r * (-LRU_C * softplus)
        a = jnp.exp(log_a)
        one_m_a2 = -jnp.tanh(log_a) * (a * a + 1.0)
        m2 = jnp.maximum(one_m_a2, 1e-12)
        mult = m2 * lax.rsqrt(m2)
        a_s[h] = a
        b_s[h] = mult * (gi * u)
        return carry

    lax.fori_loop(0, heads, coeffs, 0)

    def step(t, hs):
        tt = steps - 1 - t if reverse else t
        r0 = pl.multiple_of(tt * nb, nb)
        new = []
        for h in range(heads):
            v = a_s[h, pl.ds(r0, nb), :] * hs[h] + b_s[h, pl.ds(r0, nb), :]
            if natural_out:
                hbuf[h, pl.ds(r0, nb), :] = v + hprev_ref[h, pl.ds(r0, nb), :]
            else:
                o_ref[h, pl.ds(r0, nb), :] = v
            new.append(v)
        return tuple(new)

    hs = lax.fori_loop(0, steps, step, tuple(hl_ref[h] for h in range(heads)), unroll=8)
    for h in range(heads):
        hl_ref[h] = hs[h]

    if natural_out:
        for h in range(heads):
            for b in range(nb):
                o_ref[b, :, h * hd:(h + 1) * hd] = hbuf[h, pl.ds(b, steps, stride=nb), :]


def _lru_pass(zx, lru_p, h0, h_prev, *, reverse, steps):
    conv_w, conv_b, wg, ga_b, gx_b, lam = lru_p
    nb, seq, r = zx.shape
    heads, hd = wg.shape[0], wg.shape[1]
    assert nb == SUBLANES and hd == LANES and seq % steps == 0 and steps % SUBLANES == 0
    rows = steps * nb
    n = seq // steps
    natural_out = h_prev is not None
    chunk = (lambda i: n - 1 - i) if reverse else (lambda i: i)
    per_head = lambda a: a.reshape(-1, heads, hd).transpose(1, 0, 2)
    full = lambda shape: pl.BlockSpec(shape, lambda i: (0,) * len(shape))
    tpb = steps // SUBLANES
    in_specs = [
        pl.BlockSpec((nb, SUBLANES, r), lambda i: (0, jnp.maximum(chunk(i) * tpb - 1, 0), 0)),
        pl.BlockSpec((nb, steps, r), lambda i: (0, chunk(i), 0)),
        pl.BlockSpec((nb, SUBLANES, r), lambda i: (0, jnp.minimum((chunk(i) + 1) * tpb, seq // SUBLANES - 1), 0)),
        full((heads, LRU_CONV_W, hd)), full((heads, 1, hd)), full(wg.shape), full((heads, 1, hd)),
        full((heads, 1, hd)), full((heads, 1, hd)), full((heads, nb, hd)),
    ]
    args = [zx, zx, zx, per_head(conv_w), per_head(conv_b), wg, per_head(ga_b), per_head(gx_b), per_head(lam), h0]
    slab = pl.BlockSpec((heads, rows, hd), lambda i: (0, chunk(i), 0))
    scratch = [pltpu.VMEM((heads, rows + 2 * SUBLANES * nb, hd), F32), pltpu.VMEM((heads, rows, hd), F32),
               pltpu.VMEM((heads, rows, hd), F32)]
    if natural_out:
        in_specs.append(slab)
        args.append(h_prev)
        out0 = jax.ShapeDtypeStruct((nb, seq, r), F32)
        out_spec0 = pl.BlockSpec((nb, steps, r), lambda i: (0, chunk(i), 0))
        scratch.append(pltpu.VMEM((heads, rows, hd), F32))
    else:
        out0 = jax.ShapeDtypeStruct((heads, seq * nb, hd), F32)
        out_spec0 = slab
    blk = _nbytes((rows, r), F32)
    vmem = 12 * blk + 2 * _nbytes(wg.shape, wg.dtype)
    return pl.pallas_call(
        functools.partial(_lru_kernel, reverse=reverse, natural_out=natural_out, steps=steps),
        out_shape=[out0, jax.ShapeDtypeStruct((heads, nb, hd), F32)],
        grid=(n,),
        in_specs=in_specs,
        out_specs=[out_spec0, full((heads, nb, hd))],
        scratch_shapes=scratch,
        compiler_params=_params(("arbitrary",), vmem),
        name="lru_reverse" if reverse else "lru_forward",
    )(*args)


def _cmul(re, im, c, s):
    return re * c + im * s, im * c - re * s


def _fft_lists(re, im):
    n = len(re)
    if n == 1:
        return re, im
    er, ei = _fft_lists(re[0::2], im[0::2])
    qr, qi = _fft_lists(re[1::2], im[1::2])
    out_r, out_i = [None] * n, [None] * n
    for k in range(n // 2):
        ang = 2.0 * math.pi * k / n
        if k == 0:
            tr, ti = qr[k], qi[k]
        elif 4 * k == n:
            tr, ti = qi[k], -qr[k]
        else:
            tr, ti = _cmul(qr[k], qi[k], math.cos(ang), math.sin(ang))
        out_r[k], out_i[k] = er[k] + tr, ei[k] + ti
        out_r[k + n // 2], out_i[k + n // 2] = er[k] - tr, ei[k] - ti
    return out_r, out_i


def _dft_kernel(x_ref, w_ref, twc_ref, tws_ref, u_ref, v_ref, a_s):
    rdx = DFT_RADIX
    seq = x_ref.shape[1]
    m = seq // rdx
    w = w_ref[...].astype(BF16)
    for s1 in range(0, rdx, 2):
        xa = x_ref[0, pl.ds(s1, m, stride=rdx), :]
        xb = x_ref[0, pl.ds(s1 + 1, m, stride=rdx), :]
        xx = jnp.concatenate([xa, xb], axis=1).astype(BF16)
        a = jnp.dot(w, xx, preferred_element_type=F32)
        a_s[s1] = a[:, :LANES]
        a_s[s1 + 1] = a[:, LANES:]

    blk = 128

    def rows(t, carry):
        r0 = pl.multiple_of(t * blk, blk)
        re, im = [], []
        for s1 in range(rdx):
            ar = a_s[s1, pl.ds(r0, blk), :]
            ai = a_s[s1, pl.ds(m + r0, blk), :]
            if s1:
                ar, ai = _cmul(ar, ai, twc_ref[s1, pl.ds(r0, blk), :], tws_ref[s1, pl.ds(r0, blk), :])
            re.append(ar)
            im.append(ai)
        yr, yi = _fft_lists(re, im)
        for k1 in range(rdx):
            u_ref[0, pl.ds(k1 * m + r0, blk), :] = yr[k1].astype(u_ref.dtype)
            v_ref[0, pl.ds(k1 * m + r0, blk), :] = (-yi[k1]).astype(v_ref.dtype)
        return carry

    lax.fori_loop(0, m // blk, rows, 0)


def _position_dft(x):
    bsz, seq, c = x.shape
    rdx = DFT_RADIX
    m = seq // rdx
    k = np.arange(m)
    ang = 2.0 * np.pi * np.outer(k, k) / m
    w = jnp.asarray(np.concatenate([np.cos(ang), -np.sin(ang)], axis=0), F32)
    tw = 2.0 * np.pi * np.outer(np.arange(rdx), k) / seq
    bc = lambda t: jnp.asarray(np.broadcast_to(t[:, :, None], (rdx, m, LANES)), F32)
    tok = pl.BlockSpec((1, seq, LANES), lambda b, j: (b, 0, j))
    vmem = 2 * (3 * _nbytes((seq, LANES), F32) + _nbytes((2 * m, m), F32) + 2 * _nbytes((rdx, m, LANES), F32)) \
        + _nbytes((rdx, 2 * m, LANES), F32) + 16 * _nbytes((2 * m, LANES), F32)
    return pl.pallas_call(
        _dft_kernel,
        out_shape=[jax.ShapeDtypeStruct((bsz, seq, c), BF16)] * 2,
        grid=(bsz, c // LANES),
        in_specs=[
            tok,
            pl.BlockSpec((2 * m, m), lambda b, j: (0, 0)),
            pl.BlockSpec((rdx, m, LANES), lambda b, j: (0, 0, 0)),
            pl.BlockSpec((rdx, m, LANES), lambda b, j: (0, 0, 0)),
        ],
        out_specs=[tok, tok],
        scratch_shapes=[pltpu.VMEM((rdx, 2 * m, LANES), F32)],
        compiler_params=_params(("parallel", "parallel"), vmem),
        name="position_dft",
    )(x, w, bc(np.cos(tw)), bc(np.sin(tw)))


def _fold_kernel(cc_ref, sc_ref, w_ref, o_ref):
    w = w_ref[...]
    o_ref[0] = _dot3(cc_ref[...], w).astype(o_ref.dtype)
    o_ref[1] = (-_dot3(sc_ref[...], w)).astype(o_ref.dtype)


def _fold_channel_dft(w_fourier, seq):
    f, d = w_fourier.shape
    gd = f // FOURIER_GROUPS
    k = np.arange(gd)
    ang = 2.0 * np.pi * np.outer(k, k) / gd
    scale = 1.0 / math.sqrt(seq * gd)
    out = pl.pallas_call(
        _fold_kernel,
        out_shape=jax.ShapeDtypeStruct((2, f, d), BF16),
        grid=(FOURIER_GROUPS,),
        in_specs=[
            pl.BlockSpec((gd, gd), lambda g: (0, 0)),
            pl.BlockSpec((gd, gd), lambda g: (0, 0)),
            pl.BlockSpec((gd, d), lambda g: (g, 0)),
        ],
        out_specs=pl.BlockSpec((2, gd, d), lambda g: (0, g, 0)),
        compiler_params=_params(("parallel",), 32 * _nbytes((gd, d), F32)),
        name="fold_channel_dft",
    )(jnp.asarray(np.cos(ang) * scale, F32), jnp.asarray(np.sin(ang) * scale, F32), w_fourier)
    return out.reshape(2 * f, d)


def _merge_kernel(u_ref, v_ref, hs_ref, zy_ref, gf_ref, gr_ref, x_ref, g1_ref, sh2_ref, sc2_ref, n2_ref,
                  wcs_ref, wr_ref, wo_ref, x1_ref, h2_ref):
    uv = jnp.concatenate([u_ref[0], v_ref[0]], axis=1)
    f = jnp.dot(uv, wcs_ref[...], preferred_element_type=F32)
    p = (hs_ref[0] * _gelu_tanh(zy_ref[0].astype(F32))).astype(BF16)
    r = jnp.dot(p, wr_ref[...], preferred_element_type=F32)
    m = _sigmoid(gf_ref[0].astype(F32)) * f + _sigmoid(gr_ref[0].astype(F32)) * r
    o = jnp.dot(m.astype(BF16), wo_ref[...], preferred_element_type=F32)
    x1 = x_ref[0] + g1_ref[0] * o
    x1_ref[0] = x1
    h2_ref[0] = _rms_modulate(x1, n2_ref[...], sh2_ref[0], sc2_ref[0]).astype(h2_ref.dtype)


def _merge(u, v, hsum, zy, zg, x, mods, norm2_g, wcs, w_r, w_o, ts):
    bsz, seq, d = x.shape
    tok = lambda c, k=0: pl.BlockSpec((1, ts, c), lambda b, i: (b, i, k))
    mod = lambda k: pl.BlockSpec((1, 1, d), lambda b, i: (b, 0, k))
    full = lambda a: pl.BlockSpec(a.shape, lambda b, i: (0, 0))
    vmem = 2 * sum(_nbytes(a.shape, a.dtype) for a in (wcs, w_r, w_o)) + 24 * _nbytes((ts, d), F32)
    return pl.pallas_call(
        _merge_kernel,
        out_shape=[jax.ShapeDtypeStruct((bsz, seq, d), F32), jax.ShapeDtypeStruct((bsz, seq, d), BF16)],
        grid=(bsz, seq // ts),
        in_specs=[
            tok(u.shape[2]), tok(v.shape[2]), tok(hsum.shape[2]), tok(zy.shape[2]), tok(d, 0), tok(d, 1),
            tok(d), mod(2), mod(3), mod(4),
            pl.BlockSpec((1, d), lambda b, i: (0, 0)),
            full(wcs), full(w_r), full(w_o),
        ],
        out_specs=[tok(d), tok(d)],
        compiler_params=_params(("parallel", "parallel"), vmem),
        name="merge_mixers",
    )(u, v, hsum, zy, zg, zg, x, mods, mods, mods, norm2_g.reshape(1, d), wcs, w_r, w_o)


def _ffn_kernel(hp_ref, hm_ref, hn_ref, wg_ref, wv_ref, wgn_ref, wvn_ref, cwg_ref, cwv_ref, cbg_ref, cbv_ref,
                wd_ref, x1_ref, g2_ref, fg_ref, o_ref, ua_s, ub_s, p_s):
    part = pl.program_id(1)
    nparts = pl.num_programs(1)
    j = pl.program_id(2)
    nch = pl.num_programs(2)
    rows, d = hm_ref.shape[1], hm_ref.shape[2]
    pairs = wd_ref.shape[0] // LANES
    pad = SUBLANES + GRID_W
    tiles_per_row = GRID_W // SUBLANES
    nblk = rows // PIPE_ROWS
    gpb = PIPE_ROWS // GRID_W
    head = 2
    assert nblk >= head + 2
    sub = lax.broadcasted_iota(jnp.int32, (SUBLANES, LANES), 0)

    def w_pair(wrefs, k):
        ln = slice(k * LANES, (k + 1) * LANES)
        return jnp.concatenate([wrefs[0][:, ln], wrefs[1][:, ln]], axis=1)

    def up_halo(u_s, wrefs):
        for k in range(pairs):
            w = w_pair(wrefs, k)
            top = jnp.where(part > 0, jnp.dot(hp_ref[0], w, preferred_element_type=F32), 0.0)
            bot = jnp.where(part < nparts - 1, jnp.dot(hn_ref[0], w, preferred_element_type=F32), 0.0)
            for gv in range(2):
                u_s[gv * pairs + k, SUBLANES:pad, :] = top[:, gv * LANES:(gv + 1) * LANES]
                u_s[gv * pairs + k, pad + rows:pad + rows + GRID_W, :] = bot[:, gv * LANES:(gv + 1) * LANES]
        return [u_s[k, pad + rows:pad + rows + SUBLANES, :] for k in range(pairs)]

    def up_block(u_s, wrefs, t):
        r0 = t * PIPE_ROWS
        for k in range(pairs):
            u = jnp.dot(hm_ref[0, r0:r0 + PIPE_ROWS, :], w_pair(wrefs, k), preferred_element_type=F32)
            for gv in range(2):
                u_s[gv * pairs + k, pad + r0:pad + r0 + PIPE_ROWS, :] = u[:, gv * LANES:(gv + 1) * LANES]
        return [u_s[k, pad + r0:pad + r0 + SUBLANES, :] for k in range(pairs)]

    def down_block(t):
        r0 = t * PIPE_ROWS
        p = p_s[r0:r0 + PIPE_ROWS, :]
        for n0 in range(0, d, 2 * LANES):
            o_ref[0, r0:r0 + PIPE_ROWS, n0:n0 + 2 * LANES] += jnp.dot(p, wd_ref[:, n0:n0 + 2 * LANES],
                                                                         preferred_element_type=F32)
        return [o_ref[0, r0:r0 + SUBLANES, n0:n0 + LANES] for n0 in range(0, d, 2 * LANES)]

    def zero_after(witnesses):
        bits = pltpu.bitcast(witnesses[0], jnp.uint32)
        for w in witnesses[1:]:
            bits = bits | pltpu.bitcast(w, jnp.uint32)
        half = jnp.uint32(16)
        bits = lax.shift_right_logical(lax.shift_right_logical(bits, half), half)
        return pltpu.bitcast(bits, F32)

    def conv_block(u_s, t, zero):
        cw = (cwg_ref, cwv_ref)
        cb = (cbg_ref, cbv_ref)
        for k in range(pairs):
            ln = slice(k * LANES, (k + 1) * LANES)
            wt = [[[jnp.broadcast_to(cw[gv][3 * dh + dw:3 * dh + dw + 1, ln], (SUBLANES, LANES))
                    for dw in range(3)] for dh in range(3)] for gv in range(2)]
            bias = [jnp.broadcast_to(cb[gv][:, ln], (SUBLANES, LANES)) for gv in range(2)]
            if zero is not None:
                bias = [b + zero for b in bias]
            for g in range(t * gpb, (t + 1) * gpb):
                g0 = g * GRID_W
                gated = []
                for tile in range(tiles_per_row):
                    base = pad + g0 + tile * SUBLANES
                    acc = []
                    for gv in range(2):
                        cols = []
                        for dw in range(3):
                            col = None
                            for dh in range(3):
                                off = (dh - 1) * GRID_W + (dw - 1)
                                term = wt[gv][dh][dw] * u_s[gv * pairs + k, base + off:base + off + SUBLANES, :]
                                col = term if col is None else col + term
                            cols.append(col)
                        if tile == 0:
                            cols[0] = jnp.where(sub == 0, 0.0, cols[0])
                        if tile == tiles_per_row - 1:
                            cols[2] = jnp.where(sub == SUBLANES - 1, 0.0, cols[2])
                        acc.append((bias[gv] + cols[1]) + (cols[0] + cols[2]))
                    gated.append(_gelu_tanh(acc[0]) * acc[1])
                p_s[g0:g0 + GRID_W, ln] = jnp.concatenate(gated, axis=0).astype(p_s.dtype)

    cur_w = (wg_ref, wv_ref)
    nxt_w = (wgn_ref, wvn_ref)

    @pl.when(j == 0)
    def _():
        zeros = jnp.zeros((SUBLANES, LANES), F32)
        for u_s in (ua_s, ub_s):
            for s in range(2 * pairs):
                u_s[s, 0:SUBLANES, :] = zeros
                u_s[s, pad + rows + GRID_W:pad + rows + GRID_W + SUBLANES, :] = zeros
        o_ref[...] = jnp.zeros(o_ref.shape, o_ref.dtype)
        up_halo(ua_s, cur_w)
        for t in range(head):
            up_block(ua_s, cur_w, t)

    def step(u_cur, u_nxt):
        ahead = [functools.partial(up_block, u_cur, cur_w, t) for t in range(head, nblk)]
        ahead += [functools.partial(up_halo, u_nxt, nxt_w)]
        ahead += [functools.partial(up_block, u_nxt, nxt_w, t) for t in range(head)]
        per_slot = -(-len(ahead) // (nblk - 1))
        zero = None
        for t in range(nblk):
            witnesses = []
            for fn in ahead[:per_slot]:
                witnesses += fn()
            ahead = ahead[per_slot:]
            if t >= 1:
                witnesses += down_block(t - 1)
            conv_block(u_cur, t, zero)
            zero = zero_after(witnesses) if witnesses else None
        assert not ahead
        down_block(nblk - 1)

    @pl.when(j % 2 == 0)
    def _():
        step(ua_s, ub_s)

    @pl.when(j % 2 == 1)
    def _():
        step(ub_s, ua_s)

    @pl.when(j == nch - 1)
    def _():
        blk = 256

        def finish(t, carry):
            r0 = pl.multiple_of(t * blk, blk)
            x2 = x1_ref[0, pl.ds(r0, blk), :] + g2_ref[0] * o_ref[0, pl.ds(r0, blk), :]
            ms = jnp.mean(x2 * x2, axis=-1, keepdims=True)
            o_ref[0, pl.ds(r0, blk), :] = x2 * lax.rsqrt(ms + RMS_EPS) * fg_ref[...]
            return carry

        lax.fori_loop(0, rows // blk, finish, 0)


def _conv_ffn_final(h2, x1, mods, final_g, w_up, conv_w, conv_b, w_down, *, ck):
    bsz, seq, d = h2.shape
    d_ff = w_down.shape[0]
    nch = d_ff // ck
    pairs = ck // LANES
    nparts = 2
    rows = seq // nparts
    gpr = rows // GRID_W
    cw = conv_w.reshape(9, 2 * d_ff)
    cb = conv_b.reshape(1, 2 * d_ff)
    u_rows = rows + 2 * (GRID_W + SUBLANES)
    nxt = lambda j: jnp.minimum(j + 1, nch - 1)
    gate = lambda shape, f=(lambda j: j): pl.BlockSpec(shape, lambda b, p, j: (0, f(j)))
    value = lambda shape, f=(lambda j: j): pl.BlockSpec(shape, lambda b, p, j: (0, nch + f(j)))
    img = pl.BlockSpec((1, rows, d), lambda b, p, j: (b, p, 0))
    img_once = pl.BlockSpec((1, rows, d), lambda b, p, j: (b, p, 0), pipeline_mode=pl.Buffered(1))
    vmem = (_nbytes((rows + 4 * GRID_W, d), BF16) + 4 * _nbytes((rows, d), F32)
            + 2 * _nbytes((2 * pairs, u_rows, LANES), F32) + _nbytes((rows, ck), BF16)
            + 8 * _nbytes((d, ck), BF16) + 2 * _nbytes((ck, d), BF16) + 16 * _nbytes((PIPE_ROWS, 2 * LANES), F32))
    return pl.pallas_call(
        _ffn_kernel,
        out_shape=jax.ShapeDtypeStruct((bsz, seq, d), F32),
        grid=(bsz, nparts, nch),
        in_specs=[
            pl.BlockSpec((1, GRID_W, d), lambda b, p, j: (b, jnp.maximum(p * gpr - 1, 0), 0)),
            img_once,
            pl.BlockSpec((1, GRID_W, d), lambda b, p, j: (b, jnp.minimum((p + 1) * gpr, seq // GRID_W - 1), 0)),
            gate((d, ck)), value((d, ck)), gate((d, ck), nxt), value((d, ck), nxt),
            gate((9, ck)), value((9, ck)), gate((1, ck)), value((1, ck)),
            pl.BlockSpec((ck, d), lambda b, p, j: (j, 0)),
            img,
            pl.BlockSpec((1, 1, d), lambda b, p, j: (b, 0, 5)),
            pl.BlockSpec((1, d), lambda b, p, j: (0, 0)),
        ],
        out_specs=img,
        scratch_shapes=[
            pltpu.VMEM((2 * pairs, u_rows, LANES), F32),
            pltpu.VMEM((2 * pairs, u_rows, LANES), F32),
            pltpu.VMEM((rows, ck), BF16),
        ],
        compiler_params=_params(("parallel", "parallel", "arbitrary"), vmem),
        name="conv_ffn",
    )(h2, h2, h2, w_up, w_up, w_up, w_up, cw, cw, cb, cb, w_down, x1, mods, final_g.reshape(1, d))


def kernel(x, c, ctx, c_ctx, mod_w, mod_b, norm1_g, norm2_g, w_in, lru_conv_w, lru_conv_b, lru_ga_w, lru_ga_b,
           lru_gx_w, lru_gx_b, lru_lambda, w_fourier, w_lru_out, w_o, ffn_w_up, ffn_conv_w, ffn_conv_b,
           ffn_w_down, final_g):
    bsz, seq, d = x.shape
    ctx_len = ctx.shape[1]
    assert mod_w.shape[0] == 1 and bsz == SUBLANES
    l = 0
    f = w_fourier.shape[1]
    r = w_lru_out.shape[1]
    heads, hd = lru_ga_w.shape[2], lru_ga_w.shape[3]

    cond = jnp.concatenate([c, c_ctx[None], jnp.zeros((SUBLANES - 1, d), F32)], axis=0)
    mods = _ada_params(cond, mod_w[l], mod_b[l]).reshape(cond.shape[0], 1, N_MOD * d)

    w_in_b = w_in[l].astype(BF16)
    w_f, w_x, w_y, w_g = (w_in_b[:, :f], w_in_b[:, f:f + r], w_in_b[:, f + r:f + 2 * r], w_in_b[:, f + 2 * r:])
    wg = (0.5 * jnp.concatenate([lru_ga_w[l], lru_gx_w[l]], axis=-1)).astype(BF16)
    lru_p = lambda dr: (lru_conv_w[l], lru_conv_b[l][None], wg[dr], 0.5 * lru_ga_b[l, dr][None],
                        0.5 * lru_gx_b[l, dr][None], lru_lambda[l, dr][None])

    (zc_x,) = _in_projection(ctx, mods, lambda b: bsz, norm1_g[l], [w_x], [F32], ts=ctx_len)
    zero = jnp.zeros((heads, bsz, hd), F32)
    _, h0_f = _lru_pass(zc_x, lru_p(0), zero, None, reverse=False, steps=64)
    _, h0_b = _lru_pass(zc_x, lru_p(1), zero, None, reverse=True, steps=64)

    z_f, z_x, z_y, z_g = _in_projection(x, mods, lambda b: b, norm1_g[l], [w_f, w_x, w_y, w_g],
                                        [F32, F32, BF16, BF16], ts=512)
    h_f, _ = _lru_pass(z_x, lru_p(0), h0_f, None, reverse=False, steps=64)
    h_sum, _ = _lru_pass(z_x, lru_p(1), h0_b, h_f, reverse=True, steps=64)
    u, v = _position_dft(z_f)
    wcs = _fold_channel_dft(w_fourier[l], seq)
    x1, h2 = _merge(u, v, h_sum, z_y, z_g, x, mods, norm2_g[l], wcs,
                    w_lru_out[l].astype(BF16), w_o[l].astype(BF16), ts=512)
    return _conv_ffn_final(h2, x1, mods, final_g, ffn_w_up[l].astype(BF16), ffn_conv_w[l], ffn_conv_b[l],
                           ffn_w_down[l].astype(BF16), ck=256)
```

```python
import functools
import math

import numpy as np
import jax
import jax.numpy as jnp
from jax import lax
from jax.experimental import pallas as pl
from jax.experimental.pallas import tpu as pltpu

F32 = jnp.float32
BF16 = jnp.bfloat16

GRID_W = 64
FOURIER_GROUPS = 4
LRU_CONV_W = 4
LRU_C = 8.0
N_MOD = 6
RMS_EPS = 1e-6
DFT_RADIX = 8

V7X_VMEM_BYTES = 64 * 1024 * 1024
SUBLANES = 8
LANES = 128
PIPE_ROWS = 512


def _params(semantics, vmem_bytes):
    assert vmem_bytes < V7X_VMEM_BYTES, vmem_bytes
    return pltpu.CompilerParams(dimension_semantics=semantics, vmem_limit_bytes=int(vmem_bytes))


def _nbytes(shape, dtype):
    return math.prod(shape) * jnp.dtype(dtype).itemsize


def _sigmoid(x):
    return 0.5 * jnp.tanh(0.5 * x) + 0.5


def _gelu_tanh(x):
    c = math.sqrt(2.0 / math.pi)
    return 0.5 * x * (1.0 + jnp.tanh(c * (x + 0.044715 * (x * x * x))))


def _split_bf16(x):
    hi = x.astype(BF16)
    lo = (x - hi.astype(F32)).astype(BF16)
    return hi, lo


def _dot3(a, b):
    a_hi, a_lo = _split_bf16(a)
    b_hi, b_lo = _split_bf16(b)
    d = functools.partial(jnp.dot, preferred_element_type=F32)
    return d(a_hi, b_hi) + (d(a_hi, b_lo) + d(a_lo, b_hi))


def _rms_modulate(x, g, shift, scale):
    ms = jnp.mean(x * x, axis=-1, keepdims=True)
    y = x * lax.rsqrt(ms + RMS_EPS) * g
    return y * (1.0 + scale) + shift


def _mod_kernel(c_ref, w_ref, b_ref, o_ref):
    c = c_ref[...]
    s = c * _sigmoid(c)
    o_ref[...] = _dot3(s, w_ref[...]) + b_ref[...]


def _ada_params(cond, w, b):
    n, d = cond.shape
    width = w.shape[1]
    tn = d
    return pl.pallas_call(
        _mod_kernel,
        out_shape=jax.ShapeDtypeStruct((n, width), F32),
        grid=(width // tn,),
        in_specs=[
            pl.BlockSpec((n, d), lambda j: (0, 0)),
            pl.BlockSpec((d, tn), lambda j: (0, j)),
            pl.BlockSpec((1, tn), lambda j: (0, j)),
        ],
        out_specs=pl.BlockSpec((n, tn), lambda j: (0, j)),
        compiler_params=_params(("parallel",), 6 * _nbytes((d, tn), F32)),
        name="ada_params",
    )(cond, w, b.reshape(1, width))


def _inproj_kernel(x_ref, sh_ref, sc_ref, g_ref, *refs):
    n_out = len(refs) // 2
    h = _rms_modulate(x_ref[0], g_ref[...], sh_ref[0], sc_ref[0]).astype(BF16)
    for w_ref, o_ref in zip(refs[:n_out], refs[n_out:]):
        o_ref[0] = jnp.dot(h, w_ref[...], preferred_element_type=F32).astype(o_ref.dtype)


def _in_projection(x, mods, mod_row, norm_g, weights, out_dtypes, ts):
    bsz, seq, d = x.shape
    outs, out_specs = [], []
    for w, dt in zip(weights, out_dtypes):
        c = w.shape[1]
        outs.append(jax.ShapeDtypeStruct((bsz, seq, c), dt))
        out_specs.append(pl.BlockSpec((1, ts, c), lambda b, i: (b, i, 0)))
    w_specs = [pl.BlockSpec(w.shape, lambda b, i: (0, 0)) for w in weights]
    vmem = 2 * (_nbytes((ts, d), F32) + sum(_nbytes(w.shape, w.dtype) for w in weights)
                + sum(_nbytes((ts, w.shape[1]), F32) for w in weights)) + 4 * _nbytes((ts, d), F32)
    return pl.pallas_call(
        _inproj_kernel,
        out_shape=outs,
        grid=(bsz, seq // ts),
        in_specs=[
            pl.BlockSpec((1, ts, d), lambda b, i: (b, i, 0)),
            pl.BlockSpec((1, 1, d), lambda b, i: (mod_row(b), 0, 0)),
            pl.BlockSpec((1, 1, d), lambda b, i: (mod_row(b), 0, 1)),
            pl.BlockSpec((1, d), lambda b, i: (0, 0)),
        ] + w_specs,
        out_specs=out_specs,
        compiler_params=_params(("parallel", "parallel"), vmem),
        name="in_projection",
    )(x, mods, mods, norm_g.reshape(1, d), *weights)


def _lru_kernel(*refs, reverse, natural_out, steps):
    if natural_out:
        (zp_ref, zm_ref, zn_ref, cw_ref, cb_ref, wg_ref, gab_ref, gxb_ref, lam_ref, h0_ref, hprev_ref,
         o_ref, hl_ref, zpad, a_s, b_s, hbuf) = refs
    else:
        (zp_ref, zm_ref, zn_ref, cw_ref, cb_ref, wg_ref, gab_ref, gxb_ref, lam_ref, h0_ref,
         o_ref, hl_ref, zpad, a_s, b_s) = refs
    i = pl.program_id(0)
    n = pl.num_programs(0)
    ci = n - 1 - i if reverse else i
    nb = zm_ref.shape[0]
    heads, hd = wg_ref.shape[0], wg_ref.shape[1]
    rows = steps * nb
    halo = SUBLANES * nb

    @pl.when(i == 0)
    def _():
        hl_ref[...] = h0_ref[...]

    for h in range(heads):
        ln = slice(h * hd, (h + 1) * hd)
        for b in range(nb):
            zpad[h, pl.ds(b, SUBLANES, stride=nb), :] = jnp.where(ci > 0, zp_ref[b, :, ln], 0.0)
            zpad[h, pl.ds(halo + b, steps, stride=nb), :] = zm_ref[b, :, ln]
            zpad[h, pl.ds(halo + rows + b, SUBLANES, stride=nb), :] = jnp.where(ci < n - 1, zn_ref[b, :, ln], 0.0)

    left = LRU_CONV_W // 2

    def coeffs(h, carry):
        u = cb_ref[h]
        for k in range(LRU_CONV_W):
            u = u + cw_ref[h, k:k + 1, :] * zpad[h, pl.ds(halo + (k - left) * nb, rows), :]
        g = jnp.dot(u.astype(BF16), wg_ref[h], preferred_element_type=F32)
        r = 0.5 * jnp.tanh(g[:, :hd] + gab_ref[h]) + 0.5
        gi = 0.5 * jnp.tanh(g[:, hd:] + gxb_ref[h]) + 0.5
        nl = -lam_ref[h]
        softplus = jnp.maximum(nl, 0.0) + jnp.log1p(jnp.exp(-jnp.abs(nl)))
        log_a = r * (-LRU_C * softplus)
        a = jnp.exp(log_a)
        one_m_a2 = -jnp.tanh(log_a) * (a * a + 1.0)
        m2 = jnp.maximum(one_m_a2, 1e-12)
        mult = m2 * lax.rsqrt(m2)
        a_s[h] = a
        b_s[h] = mult * (gi * u)
        return carry

    lax.fori_loop(0, heads, coeffs, 0)

    def emit(h, r0, v):
        if natural_out:
            hbuf[h, r0:r0 + nb, :] = v + hprev_ref[h, r0:r0 + nb, :]
        else:
            o_ref[h, r0:r0 + nb, :] = v

    hs = [hl_ref[h] for h in range(heads)]
    for i in range(steps // 2):
        t0 = steps - 1 - 2 * i if reverse else 2 * i
        t1 = t0 - 1 if reverse else t0 + 1
        r0, r1 = t0 * nb, t1 * nb
        for h in range(heads):
            a0, b0 = a_s[h, r0:r0 + nb, :], b_s[h, r0:r0 + nb, :]
            a1, b1 = a_s[h, r1:r1 + nb, :], b_s[h, r1:r1 + nb, :]
            h1 = (a1 * a0) * hs[h] + (a1 * b0 + b1)
            emit(h, r0, a0 * hs[h] + b0)
            emit(h, r1, h1)
            hs[h] = h1
    for h in range(heads):
        hl_ref[h] = hs[h]

    if natural_out:
        for h in range(heads):
            for b in range(nb):
                o_ref[b, :, h * hd:(h + 1) * hd] = hbuf[h, pl.ds(b, steps, stride=nb), :]


def _lru_pass(zx, lru_p, h0, h_prev, *, reverse, steps):
    conv_w, conv_b, wg, ga_b, gx_b, lam = lru_p
    nb, seq, r = zx.shape
    heads, hd = wg.shape[0], wg.shape[1]
    assert nb == SUBLANES and hd == LANES and seq % steps == 0 and steps % SUBLANES == 0
    rows = steps * nb
    n = seq // steps
    natural_out = h_prev is not None
    chunk = (lambda i: n - 1 - i) if reverse else (lambda i: i)
    per_head = lambda a: a.reshape(-1, heads, hd).transpose(1, 0, 2)
    full = lambda shape: pl.BlockSpec(shape, lambda i: (0,) * len(shape))
    tpb = steps // SUBLANES
    in_specs = [
        pl.BlockSpec((nb, SUBLANES, r), lambda i: (0, jnp.maximum(chunk(i) * tpb - 1, 0), 0)),
        pl.BlockSpec((nb, steps, r), lambda i: (0, chunk(i), 0)),
        pl.BlockSpec((nb, SUBLANES, r), lambda i: (0, jnp.minimum((chunk(i) + 1) * tpb, seq // SUBLANES - 1), 0)),
        full((heads, LRU_CONV_W, hd)), full((heads, 1, hd)), full(wg.shape), full((heads, 1, hd)),
        full((heads, 1, hd)), full((heads, 1, hd)), full((heads, nb, hd)),
    ]
    args = [zx, zx, zx, per_head(conv_w), per_head(conv_b), wg, per_head(ga_b), per_head(gx_b), per_head(lam), h0]
    slab = pl.BlockSpec((heads, rows, hd), lambda i: (0, chunk(i), 0))
    scratch = [pltpu.VMEM((heads, rows + 2 * SUBLANES * nb, hd), F32), pltpu.VMEM((heads, rows, hd), F32),
               pltpu.VMEM((heads, rows, hd), F32)]
    if natural_out:
        in_specs.append(slab)
        args.append(h_prev)
        out0 = jax.ShapeDtypeStruct((nb, seq, r), F32)
        out_spec0 = pl.BlockSpec((nb, steps, r), lambda i: (0, chunk(i), 0))
        scratch.append(pltpu.VMEM((heads, rows, hd), F32))
    else:
        out0 = jax.ShapeDtypeStruct((heads, seq * nb, hd), F32)
        out_spec0 = slab
    blk = _nbytes((rows, r), F32)
    vmem = 12 * blk + 2 * _nbytes(wg.shape, wg.dtype)
    return pl.pallas_call(
        functools.partial(_lru_kernel, reverse=reverse, natural_out=natural_out, steps=steps),
        out_shape=[out0, jax.ShapeDtypeStruct((heads, nb, hd), F32)],
        grid=(n,),
        in_specs=in_specs,
        out_specs=[out_spec0, full((heads, nb, hd))],
        scratch_shapes=scratch,
        compiler_params=_params(("arbitrary",), vmem),
        name="lru_reverse" if reverse else "lru_forward",
    )(*args)


def _cmul(re, im, c, s):
    return re * c + im * s, im * c - re * s


def _fft_lists(re, im):
    n = len(re)
    if n == 1:
        return re, im
    er, ei = _fft_lists(re[0::2], im[0::2])
    qr, qi = _fft_lists(re[1::2], im[1::2])
    out_r, out_i = [None] * n, [None] * n
    for k in range(n // 2):
        ang = 2.0 * math.pi * k / n
        if k == 0:
            tr, ti = qr[k], qi[k]
        elif 4 * k == n:
            tr, ti = qi[k], -qr[k]
        else:
            tr, ti = _cmul(qr[k], qi[k], math.cos(ang), math.sin(ang))
        out_r[k], out_i[k] = er[k] + tr, ei[k] + ti
        out_r[k + n // 2], out_i[k + n // 2] = er[k] - tr, ei[k] - ti
    return out_r, out_i


def _dft_kernel(x_ref, w_ref, twc_ref, tws_ref, u_ref, v_ref, a_s):
    rdx = DFT_RADIX
    seq = x_ref.shape[1]
    m = seq // rdx
    w = w_ref[...].astype(BF16)
    for s1 in range(0, rdx, 2):
        xa = x_ref[0, pl.ds(s1, m, stride=rdx), :]
        xb = x_ref[0, pl.ds(s1 + 1, m, stride=rdx), :]
        xx = jnp.concatenate([xa, xb], axis=1).astype(BF16)
        a = jnp.dot(w, xx, preferred_element_type=F32)
        a_s[s1] = a[:, :LANES]
        a_s[s1 + 1] = a[:, LANES:]

    blk = 128

    def rows(t, carry):
        r0 = pl.multiple_of(t * blk, blk)
        re, im = [], []
        for s1 in range(rdx):
            ar = a_s[s1, pl.ds(r0, blk), :]
            ai = a_s[s1, pl.ds(m + r0, blk), :]
            if s1:
                ar, ai = _cmul(ar, ai, twc_ref[s1, pl.ds(r0, blk), :], tws_ref[s1, pl.ds(r0, blk), :])
            re.append(ar)
            im.append(ai)
        yr, yi = _fft_lists(re, im)
        for k1 in range(rdx):
            u_ref[0, pl.ds(k1 * m + r0, blk), :] = yr[k1].astype(u_ref.dtype)
            v_ref[0, pl.ds(k1 * m + r0, blk), :] = (-yi[k1]).astype(v_ref.dtype)
        return carry

    lax.fori_loop(0, m // blk, rows, 0)


def _position_dft(x):
    bsz, seq, c = x.shape
    rdx = DFT_RADIX
    m = seq // rdx
    k = np.arange(m)
    ang = 2.0 * np.pi * np.outer(k, k) / m
    w = jnp.asarray(np.concatenate([np.cos(ang), -np.sin(ang)], axis=0), F32)
    tw = 2.0 * np.pi * np.outer(np.arange(rdx), k) / seq
    bc = lambda t: jnp.asarray(np.broadcast_to(t[:, :, None], (rdx, m, LANES)), F32)
    tok = pl.BlockSpec((1, seq, LANES), lambda b, j: (b, 0, j))
    vmem = 2 * (3 * _nbytes((seq, LANES), F32) + _nbytes((2 * m, m), F32) + 2 * _nbytes((rdx, m, LANES), F32)) \
        + _nbytes((rdx, 2 * m, LANES), F32) + 16 * _nbytes((2 * m, LANES), F32)
    return pl.pallas_call(
        _dft_kernel,
        out_shape=[jax.ShapeDtypeStruct((bsz, seq, c), BF16)] * 2,
        grid=(bsz, c // LANES),
        in_specs=[
            tok,
            pl.BlockSpec((2 * m, m), lambda b, j: (0, 0)),
            pl.BlockSpec((rdx, m, LANES), lambda b, j: (0, 0, 0)),
            pl.BlockSpec((rdx, m, LANES), lambda b, j: (0, 0, 0)),
        ],
        out_specs=[tok, tok],
        scratch_shapes=[pltpu.VMEM((rdx, 2 * m, LANES), F32)],
        compiler_params=_params(("parallel", "parallel"), vmem),
        name="position_dft",
    )(x, w, bc(np.cos(tw)), bc(np.sin(tw)))


def _fold_kernel(cc_ref, sc_ref, w_ref, o_ref):
    w = w_ref[...]
    o_ref[0] = _dot3(cc_ref[...], w).astype(o_ref.dtype)
    o_ref[1] = (-_dot3(sc_ref[...], w)).astype(o_ref.dtype)


def _fold_channel_dft(w_fourier, seq):
    f, d = w_fourier.shape
    gd = f // FOURIER_GROUPS
    k = np.arange(gd)
    ang = 2.0 * np.pi * np.outer(k, k) / gd
    scale = 1.0 / math.sqrt(seq * gd)
    out = pl.pallas_call(
        _fold_kernel,
        out_shape=jax.ShapeDtypeStruct((2, f, d), BF16),
        grid=(FOURIER_GROUPS,),
        in_specs=[
            pl.BlockSpec((gd, gd), lambda g: (0, 0)),
            pl.BlockSpec((gd, gd), lambda g: (0, 0)),
            pl.BlockSpec((gd, d), lambda g: (g, 0)),
        ],
        out_specs=pl.BlockSpec((2, gd, d), lambda g: (0, g, 0)),
        compiler_params=_params(("parallel",), 32 * _nbytes((gd, d), F32)),
        name="fold_channel_dft",
    )(jnp.asarray(np.cos(ang) * scale, F32), jnp.asarray(np.sin(ang) * scale, F32), w_fourier)
    return out.reshape(2 * f, d)


def _merge_kernel(u_ref, v_ref, hs_ref, zy_ref, gf_ref, gr_ref, x_ref, g1_ref, sh2_ref, sc2_ref, n2_ref,
                  wcs_ref, wr_ref, wo_ref, x1_ref, h2_ref):
    uv = jnp.concatenate([u_ref[0], v_ref[0]], axis=1)
    f = jnp.dot(uv, wcs_ref[...], preferred_element_type=F32)
    p = (hs_ref[0] * _gelu_tanh(zy_ref[0].astype(F32))).astype(BF16)
    r = jnp.dot(p, wr_ref[...], preferred_element_type=F32)
    m = _sigmoid(gf_ref[0].astype(F32)) * f + _sigmoid(gr_ref[0].astype(F32)) * r
    o = jnp.dot(m.astype(BF16), wo_ref[...], preferred_element_type=F32)
    x1 = x_ref[0] + g1_ref[0] * o
    x1_ref[0] = x1
    h2_ref[0] = _rms_modulate(x1, n2_ref[...], sh2_ref[0], sc2_ref[0]).astype(h2_ref.dtype)


def _merge(u, v, hsum, zy, zg, x, mods, norm2_g, wcs, w_r, w_o, ts):
    bsz, seq, d = x.shape
    tok = lambda c, k=0: pl.BlockSpec((1, ts, c), lambda b, i: (b, i, k))
    mod = lambda k: pl.BlockSpec((1, 1, d), lambda b, i: (b, 0, k))
    full = lambda a: pl.BlockSpec(a.shape, lambda b, i: (0, 0))
    vmem = 2 * sum(_nbytes(a.shape, a.dtype) for a in (wcs, w_r, w_o)) + 24 * _nbytes((ts, d), F32)
    return pl.pallas_call(
        _merge_kernel,
        out_shape=[jax.ShapeDtypeStruct((bsz, seq, d), F32), jax.ShapeDtypeStruct((bsz, seq, d), BF16)],
        grid=(bsz, seq // ts),
        in_specs=[
            tok(u.shape[2]), tok(v.shape[2]), tok(hsum.shape[2]), tok(zy.shape[2]), tok(d, 0), tok(d, 1),
            tok(d), mod(2), mod(3), mod(4),
            pl.BlockSpec((1, d), lambda b, i: (0, 0)),
            full(wcs), full(w_r), full(w_o),
        ],
        out_specs=[tok(d), tok(d)],
        compiler_params=_params(("parallel", "parallel"), vmem),
        name="merge_mixers",
    )(u, v, hsum, zy, zg, zg, x, mods, mods, mods, norm2_g.reshape(1, d), wcs, w_r, w_o)


def _ffn_kernel(hp_ref, hm_ref, hn_ref, wg_ref, wv_ref, wgn_ref, wvn_ref, cwg_ref, cwv_ref, cbg_ref, cbv_ref,
                wd_ref, x1_ref, g2_ref, fg_ref, o_ref, u_s, p_s):
    part = pl.program_id(1)
    nparts = pl.num_programs(1)
    j = pl.program_id(2)
    nch = pl.num_programs(2)
    rows, d = hm_ref.shape[1], hm_ref.shape[2]
    pairs = wd_ref.shape[0] // LANES
    pad = SUBLANES + GRID_W
    tiles_per_row = GRID_W // SUBLANES
    nblk = rows // PIPE_ROWS
    gpb = PIPE_ROWS // GRID_W
    head = 2
    assert nblk >= head + 2
    sub = lax.broadcasted_iota(jnp.int32, (SUBLANES, LANES), 0)

    def w_pair(wrefs, k):
        ln = slice(k * LANES, (k + 1) * LANES)
        return jnp.concatenate([wrefs[0][:, ln], wrefs[1][:, ln]], axis=1)

    def up_halo(u_s, wrefs, h_ref, inside, r0):
        for k in range(pairs):
            u = jnp.where(inside, jnp.dot(h_ref[0], w_pair(wrefs, k), preferred_element_type=F32), 0.0)
            for gv in range(2):
                u_s[gv * pairs + k, r0:r0 + GRID_W, :] = u[:, gv * LANES:(gv + 1) * LANES]
        return [u_s[k, r0:r0 + SUBLANES, :] for k in range(pairs)]

    up_halo_top = lambda u_s, wrefs: up_halo(u_s, wrefs, hp_ref, part > 0, SUBLANES)
    up_halo_bot = lambda u_s, wrefs: up_halo(u_s, wrefs, hn_ref, part < nparts - 1, pad + rows)

    def up_block(u_s, wrefs, t):
        r0 = t * PIPE_ROWS
        for k in range(pairs):
            u = jnp.dot(hm_ref[0, r0:r0 + PIPE_ROWS, :], w_pair(wrefs, k), preferred_element_type=F32)
            for gv in range(2):
                u_s[gv * pairs + k, pad + r0:pad + r0 + PIPE_ROWS, :] = u[:, gv * LANES:(gv + 1) * LANES]
        return [u_s[k, pad + r0:pad + r0 + SUBLANES, :] for k in range(pairs)]

    def down_block(t):
        r0 = t * PIPE_ROWS
        p = p_s[r0:r0 + PIPE_ROWS, :]
        for n0 in range(0, d, 2 * LANES):
            o_ref[0, r0:r0 + PIPE_ROWS, n0:n0 + 2 * LANES] += jnp.dot(p, wd_ref[:, n0:n0 + 2 * LANES],
                                                                         preferred_element_type=F32)
        return [o_ref[0, r0:r0 + SUBLANES, n0:n0 + LANES] for n0 in range(0, d, 2 * LANES)]

    def zero_after(witnesses):
        bits = pltpu.bitcast(witnesses[0], jnp.uint32)
        for w in witnesses[1:]:
            bits = bits | pltpu.bitcast(w, jnp.uint32)
        half = jnp.uint32(16)
        bits = lax.shift_right_logical(lax.shift_right_logical(bits, half), half)
        return pltpu.bitcast(bits, F32)

    def conv_block(u_s, t, zero):
        cw = (cwg_ref, cwv_ref)
        cb = (cbg_ref, cbv_ref)
        for k in range(pairs):
            ln = slice(k * LANES, (k + 1) * LANES)
            wt = [[[jnp.broadcast_to(cw[gv][3 * dh + dw:3 * dh + dw + 1, ln], (SUBLANES, LANES))
                    for dw in range(3)] for dh in range(3)] for gv in range(2)]
            bias = [jnp.broadcast_to(cb[gv][:, ln], (SUBLANES, LANES)) for gv in range(2)]
            if zero is not None:
                bias = [b + zero for b in bias]
            for g in range(t * gpb, (t + 1) * gpb):
                g0 = g * GRID_W
                gated = []
                for tile in range(tiles_per_row):
                    base = pad + g0 + tile * SUBLANES
                    acc = []
                    for gv in range(2):
                        cols = []
                        for dw in range(3):
                            col = None
                            for dh in range(3):
                                off = (dh - 1) * GRID_W + (dw - 1)
                                term = wt[gv][dh][dw] * u_s[gv * pairs + k, base + off:base + off + SUBLANES, :]
                                col = term if col is None else col + term
                            cols.append(col)
                        if tile == 0:
                            cols[0] = jnp.where(sub == 0, 0.0, cols[0])
                        if tile == tiles_per_row - 1:
                            cols[2] = jnp.where(sub == SUBLANES - 1, 0.0, cols[2])
                        acc.append((bias[gv] + cols[1]) + (cols[0] + cols[2]))
                    gated.append(_gelu_tanh(acc[0]) * acc[1])
                p_s[g0:g0 + GRID_W, ln] = jnp.concatenate(gated, axis=0).astype(p_s.dtype)

    cur_w = (wg_ref, wv_ref)
    nxt_w = (wgn_ref, wvn_ref)

    @pl.when(j == 0)
    def _():
        zeros = jnp.zeros((SUBLANES, LANES), F32)
        for s in range(2 * pairs):
            u_s[s, 0:SUBLANES, :] = zeros
            u_s[s, pad + rows + GRID_W:pad + rows + GRID_W + SUBLANES, :] = zeros
        o_ref[...] = jnp.zeros(o_ref.shape, o_ref.dtype)
        up_halo_top(u_s, cur_w)
        for t in range(head):
            up_block(u_s, cur_w, t)

    slots = [[] for _ in range(nblk)]
    slots[0].append(functools.partial(up_halo_bot, u_s, cur_w))
    for t in range(head, nblk):
        slots[t - head].append(functools.partial(up_block, u_s, cur_w, t))
    slots[nblk - head].append(functools.partial(up_halo_top, u_s, nxt_w))
    for i in range(head):
        assert nblk - head + i >= i + 2
        slots[nblk - head + i].append(functools.partial(up_block, u_s, nxt_w, i))
    zero = None
    for t in range(nblk):
        witnesses = []
        for fn in slots[t]:
            witnesses += fn()
        if t >= 1:
            witnesses += down_block(t - 1)
        conv_block(u_s, t, zero)
        zero = zero_after(witnesses)
    down_block(nblk - 1)

    @pl.when(j == nch - 1)
    def _():
        blk = 256

        def finish(t, carry):
            r0 = pl.multiple_of(t * blk, blk)
            x2 = x1_ref[0, pl.ds(r0, blk), :] + g2_ref[0] * o_ref[0, pl.ds(r0, blk), :]
            ms = jnp.mean(x2 * x2, axis=-1, keepdims=True)
            o_ref[0, pl.ds(r0, blk), :] = x2 * lax.rsqrt(ms + RMS_EPS) * fg_ref[...]
            return carry

        lax.fori_loop(0, rows // blk, finish, 0)


def _conv_ffn_final(h2, x1, mods, final_g, w_up, conv_w, conv_b, w_down, *, ck):
    bsz, seq, d = h2.shape
    d_ff = w_down.shape[0]
    nch = d_ff // ck
    pairs = ck // LANES
    nparts = 2
    rows = seq // nparts
    gpr = rows // GRID_W
    cw = conv_w.reshape(9, 2 * d_ff)
    cb = conv_b.reshape(1, 2 * d_ff)
    u_rows = rows + 2 * (GRID_W + SUBLANES)
    nxt = lambda j: jnp.minimum(j + 1, nch - 1)
    gate = lambda shape, f=(lambda j: j): pl.BlockSpec(shape, lambda b, p, j: (0, f(j)))
    value = lambda shape, f=(lambda j: j): pl.BlockSpec(shape, lambda b, p, j: (0, nch + f(j)))
    img = pl.BlockSpec((1, rows, d), lambda b, p, j: (b, p, 0))
    img_once = pl.BlockSpec((1, rows, d), lambda b, p, j: (b, p, 0), pipeline_mode=pl.Buffered(1))
    vmem = (_nbytes((rows + 4 * GRID_W, d), BF16) + 4 * _nbytes((rows, d), F32)
            + _nbytes((2 * pairs, u_rows, LANES), F32) + _nbytes((rows, ck), BF16)
            + 8 * _nbytes((d, ck), BF16) + 2 * _nbytes((ck, d), BF16) + 16 * _nbytes((PIPE_ROWS, 2 * LANES), F32))
    return pl.pallas_call(
        _ffn_kernel,
        out_shape=jax.ShapeDtypeStruct((bsz, seq, d), F32),
        grid=(bsz, nparts, nch),
        in_specs=[
            pl.BlockSpec((1, GRID_W, d), lambda b, p, j: (b, jnp.maximum(p * gpr - 1, 0), 0)),
            img_once,
            pl.BlockSpec((1, GRID_W, d), lambda b, p, j: (b, jnp.minimum((p + 1) * gpr, seq // GRID_W - 1), 0)),
            gate((d, ck)), value((d, ck)), gate((d, ck), nxt), value((d, ck), nxt),
            gate((9, ck)), value((9, ck)), gate((1, ck)), value((1, ck)),
            pl.BlockSpec((ck, d), lambda b, p, j: (j, 0)),
            img,
            pl.BlockSpec((1, 1, d), lambda b, p, j: (b, 0, 5)),
            pl.BlockSpec((1, d), lambda b, p, j: (0, 0)),
        ],
        out_specs=img,
        scratch_shapes=[
            pltpu.VMEM((2 * pairs, u_rows, LANES), F32),
            pltpu.VMEM((rows, ck), BF16),
        ],
        compiler_params=_params(("parallel", "parallel", "arbitrary"), vmem),
        name="conv_ffn",
    )(h2, h2, h2, w_up, w_up, w_up, w_up, cw, cw, cb, cb, w_down, x1, mods, final_g.reshape(1, d))


def kernel(x, c, ctx, c_ctx, mod_w, mod_b, norm1_g, norm2_g, w_in, lru_conv_w, lru_conv_b, lru_ga_w, lru_ga_b,
           lru_gx_w, lru_gx_b, lru_lambda, w_fourier, w_lru_out, w_o, ffn_w_up, ffn_conv_w, ffn_conv_b,
           ffn_w_down, final_g):
    bsz, seq, d = x.shape
    ctx_len = ctx.shape[1]
    assert mod_w.shape[0] == 1 and bsz == SUBLANES
    l = 0
    f = w_fourier.shape[1]
    r = w_lru_out.shape[1]
    heads, hd = lru_ga_w.shape[2], lru_ga_w.shape[3]

    cond = jnp.concatenate([c, c_ctx[None], jnp.zeros((SUBLANES - 1, d), F32)], axis=0)
    mods = _ada_params(cond, mod_w[l], mod_b[l]).reshape(cond.shape[0], 1, N_MOD * d)

    w_in_b = w_in[l].astype(BF16)
    w_f, w_x, w_y, w_g = (w_in_b[:, :f], w_in_b[:, f:f + r], w_in_b[:, f + r:f + 2 * r], w_in_b[:, f + 2 * r:])
    wg = (0.5 * jnp.concatenate([lru_ga_w[l], lru_gx_w[l]], axis=-1)).astype(BF16)
    lru_p = lambda dr: (lru_conv_w[l], lru_conv_b[l][None], wg[dr], 0.5 * lru_ga_b[l, dr][None],
                        0.5 * lru_gx_b[l, dr][None], lru_lambda[l, dr][None])

    (zc_x,) = _in_projection(ctx, mods, lambda b: bsz, norm1_g[l], [w_x], [F32], ts=ctx_len)
    zero = jnp.zeros((heads, bsz, hd), F32)
    _, h0_f = _lru_pass(zc_x, lru_p(0), zero, None, reverse=False, steps=64)
    _, h0_b = _lru_pass(zc_x, lru_p(1), zero, None, reverse=True, steps=64)

    z_f, z_x, z_y, z_g = _in_projection(x, mods, lambda b: b, norm1_g[l], [w_f, w_x, w_y, w_g],
                                        [F32, F32, BF16, BF16], ts=512)
    h_f, _ = _lru_pass(z_x, lru_p(0), h0_f, None, reverse=False, steps=64)
    h_sum, _ = _lru_pass(z_x, lru_p(1), h0_b, h_f, reverse=True, steps=64)
    u, v = _position_dft(z_f)
    wcs = _fold_channel_dft(w_fourier[l], seq)
    x1, h2 = _merge(u, v, h_sum, z_y, z_g, x, mods, norm2_g[l], wcs,
                    w_lru_out[l].astype(BF16), w_o[l].astype(BF16), ts=512)
    return _conv_ffn_final(h2, x1, mods, final_g, ffn_w_up[l].astype(BF16), ffn_conv_w[l], ffn_conv_b[l],
                           ffn_w_down[l].astype(BF16), ck=256)
```

```python
import functools
import math

import numpy as np
import jax
import jax.numpy as jnp
from jax import lax
from jax.experimental import pallas as pl
from jax.experimental.pallas import tpu as pltpu

F32 = jnp.float32
BF16 = jnp.bfloat16

GRID_W = 64
FOURIER_GROUPS = 4
LRU_CONV_W = 4
LRU_C = 8.0
N_MOD = 6
RMS_EPS = 1e-6
LN_2 = math.log(2.0)
LOG2_E = 1.0 / LN_2
DFT_RADIX = 8

V7X_VMEM_BYTES = 64 * 1024 * 1024
SUBLANES = 8
LANES = 128
PIPE_ROWS = 512


def _params(semantics, vmem_bytes):
    assert vmem_bytes < V7X_VMEM_BYTES, vmem_bytes
    return pltpu.CompilerParams(dimension_semantics=semantics, vmem_limit_bytes=int(vmem_bytes))


def _nbytes(shape, dtype):
    return math.prod(shape) * jnp.dtype(dtype).itemsize


def _sigmoid(x):
    return 0.5 * jnp.tanh(0.5 * x) + 0.5


def _gelu_tanh(x):
    c = math.sqrt(2.0 / math.pi)
    return 0.5 * x * (1.0 + jnp.tanh(c * (x + 0.044715 * (x * x * x))))


def _twice_gelu_tanh(x):
    c = math.sqrt(2.0 / math.pi)
    t = jnp.tanh(x * (c + (c * 0.044715) * (x * x)))
    return x + x * t


def _split_bf16(x):
    hi = x.astype(BF16)
    lo = (x - hi.astype(F32)).astype(BF16)
    return hi, lo


def _dot3(a, b):
    a_hi, a_lo = _split_bf16(a)
    b_hi, b_lo = _split_bf16(b)
    d = functools.partial(jnp.dot, preferred_element_type=F32)
    return d(a_hi, b_hi) + (d(a_hi, b_lo) + d(a_lo, b_hi))


def _rms_modulate(x, g, shift, scale):
    ms = jnp.mean(x * x, axis=-1, keepdims=True)
    y = x * lax.rsqrt(ms + RMS_EPS) * g
    return y * (1.0 + scale) + shift


def _mod_kernel(c_ref, w_ref, b_ref, o_ref):
    c = c_ref[...]
    s = c * _sigmoid(c)
    o_ref[...] = _dot3(s, w_ref[...]) + b_ref[...]


def _ada_params(cond, w, b):
    n, d = cond.shape
    width = w.shape[1]
    tn = d
    return pl.pallas_call(
        _mod_kernel,
        out_shape=jax.ShapeDtypeStruct((n, width), F32),
        grid=(width // tn,),
        in_specs=[
            pl.BlockSpec((n, d), lambda j: (0, 0)),
            pl.BlockSpec((d, tn), lambda j: (0, j)),
            pl.BlockSpec((1, tn), lambda j: (0, j)),
        ],
        out_specs=pl.BlockSpec((n, tn), lambda j: (0, j)),
        compiler_params=_params(("parallel",), 6 * _nbytes((d, tn), F32)),
        name="ada_params",
    )(cond, w, b.reshape(1, width))


def _inproj_kernel(x_ref, sh_ref, sc_ref, g_ref, *refs):
    n_out = len(refs) // 2
    h = _rms_modulate(x_ref[0], g_ref[...], sh_ref[0], sc_ref[0]).astype(BF16)
    for w_ref, o_ref in zip(refs[:n_out], refs[n_out:]):
        o_ref[0] = jnp.dot(h, w_ref[...], preferred_element_type=F32).astype(o_ref.dtype)


def _in_projection(x, mods, mod_row, norm_g, weights, out_dtypes, ts):
    bsz, seq, d = x.shape
    outs, out_specs = [], []
    for w, dt in zip(weights, out_dtypes):
        c = w.shape[1]
        outs.append(jax.ShapeDtypeStruct((bsz, seq, c), dt))
        out_specs.append(pl.BlockSpec((1, ts, c), lambda b, i: (b, i, 0)))
    w_specs = [pl.BlockSpec(w.shape, lambda b, i: (0, 0)) for w in weights]
    vmem = 2 * (_nbytes((ts, d), F32) + sum(_nbytes(w.shape, w.dtype) for w in weights)
                + sum(_nbytes((ts, w.shape[1]), F32) for w in weights)) + 4 * _nbytes((ts, d), F32)
    return pl.pallas_call(
        _inproj_kernel,
        out_shape=outs,
        grid=(bsz, seq // ts),
        in_specs=[
            pl.BlockSpec((1, ts, d), lambda b, i: (b, i, 0)),
            pl.BlockSpec((1, 1, d), lambda b, i: (mod_row(b), 0, 0)),
            pl.BlockSpec((1, 1, d), lambda b, i: (mod_row(b), 0, 1)),
            pl.BlockSpec((1, d), lambda b, i: (0, 0)),
        ] + w_specs,
        out_specs=out_specs,
        compiler_params=_params(("parallel", "parallel"), vmem),
        name="in_projection",
    )(x, mods, mods, norm_g.reshape(1, d), *weights)


def _lru_kernel(*refs, reverse, natural_out, steps):
    if natural_out:
        (zp_ref, zm_ref, zn_ref, cw_ref, cb_ref, wg_ref, gab_ref, gxb_ref, lam_ref, h0_ref, hprev_ref,
         o_ref, hl_ref, zpad, a_s, b_s, hbuf) = refs
    else:
        (zp_ref, zm_ref, zn_ref, cw_ref, cb_ref, wg_ref, gab_ref, gxb_ref, lam_ref, h0_ref,
         o_ref, hl_ref, zpad, a_s, b_s) = refs
    i = pl.program_id(0)
    n = pl.num_programs(0)
    ci = n - 1 - i if reverse else i
    nb = zm_ref.shape[0]
    heads, hd = wg_ref.shape[0], wg_ref.shape[1]
    rows = steps * nb
    halo = SUBLANES * nb

    @pl.when(i == 0)
    def _():
        hl_ref[...] = h0_ref[...]

    for h in range(heads):
        ln = slice(h * hd, (h + 1) * hd)
        for b in range(nb):
            zpad[h, pl.ds(b, SUBLANES, stride=nb), :] = jnp.where(ci > 0, zp_ref[b, :, ln], 0.0)
            zpad[h, pl.ds(halo + b, steps, stride=nb), :] = zm_ref[b, :, ln]
            zpad[h, pl.ds(halo + rows + b, SUBLANES, stride=nb), :] = jnp.where(ci < n - 1, zn_ref[b, :, ln], 0.0)

    left = LRU_CONV_W // 2

    def coeffs(h, carry):
        u = cb_ref[h]
        for k in range(LRU_CONV_W):
            u = u + cw_ref[h, k:k + 1, :] * zpad[h, pl.ds(halo + (k - left) * nb, rows), :]
        g = jnp.dot(u.astype(BF16), wg_ref[h], preferred_element_type=F32)
        tr = jnp.tanh(g[:, :hd] + gab_ref[h])
        gi = 0.5 * jnp.tanh(g[:, hd:] + gxb_ref[h]) + 0.5
        nl = -lam_ref[h]
        softplus = jnp.maximum(nl, 0.0) + jnp.log1p(jnp.exp(-jnp.abs(nl)))
        half = (-0.5 * LRU_C * LOG2_E) * softplus
        log2_a = tr * half + half
        a = jnp.exp2(log2_a)
        m2 = jnp.maximum(jnp.tanh(log2_a * (-LN_2)) * (a * a + 1.0), 1e-12)
        mult = m2 * lax.rsqrt(m2)
        a_s[h] = a
        b_s[h] = mult * (gi * u)
        return carry

    lax.fori_loop(0, heads, coeffs, 0)

    def emit(h, r0, v):
        if natural_out:
            hbuf[h, r0:r0 + nb, :] = v + hprev_ref[h, r0:r0 + nb, :]
        else:
            o_ref[h, r0:r0 + nb, :] = v

    hs = [hl_ref[h] for h in range(heads)]
    for i in range(steps // 2):
        t0 = steps - 1 - 2 * i if reverse else 2 * i
        t1 = t0 - 1 if reverse else t0 + 1
        r0, r1 = t0 * nb, t1 * nb
        for h in range(heads):
            a0, b0 = a_s[h, r0:r0 + nb, :], b_s[h, r0:r0 + nb, :]
            a1, b1 = a_s[h, r1:r1 + nb, :], b_s[h, r1:r1 + nb, :]
            h1 = (a1 * a0) * hs[h] + (a1 * b0 + b1)
            emit(h, r0, a0 * hs[h] + b0)
            emit(h, r1, h1)
            hs[h] = h1
    for h in range(heads):
        hl_ref[h] = hs[h]

    if natural_out:
        for h in range(heads):
            for b in range(nb):
                o_ref[b, :, h * hd:(h + 1) * hd] = hbuf[h, pl.ds(b, steps, stride=nb), :]


def _lru_pass(zx, lru_p, h0, h_prev, *, reverse, steps):
    conv_w, conv_b, wg, ga_b, gx_b, lam = lru_p
    nb, seq, r = zx.shape
    heads, hd = wg.shape[0], wg.shape[1]
    assert nb == SUBLANES and hd == LANES and seq % steps == 0 and steps % SUBLANES == 0
    rows = steps * nb
    n = seq // steps
    natural_out = h_prev is not None
    chunk = (lambda i: n - 1 - i) if reverse else (lambda i: i)
    per_head = lambda a: a.reshape(-1, heads, hd).transpose(1, 0, 2)
    full = lambda shape: pl.BlockSpec(shape, lambda i: (0,) * len(shape))
    tpb = steps // SUBLANES
    in_specs = [
        pl.BlockSpec((nb, SUBLANES, r), lambda i: (0, jnp.maximum(chunk(i) * tpb - 1, 0), 0)),
        pl.BlockSpec((nb, steps, r), lambda i: (0, chunk(i), 0)),
        pl.BlockSpec((nb, SUBLANES, r), lambda i: (0, jnp.minimum((chunk(i) + 1) * tpb, seq // SUBLANES - 1), 0)),
        full((heads, LRU_CONV_W, hd)), full((heads, 1, hd)), full(wg.shape), full((heads, 1, hd)),
        full((heads, 1, hd)), full((heads, 1, hd)), full((heads, nb, hd)),
    ]
    args = [zx, zx, zx, per_head(conv_w), per_head(conv_b), wg, per_head(ga_b), per_head(gx_b), per_head(lam), h0]
    slab = pl.BlockSpec((heads, rows, hd), lambda i: (0, chunk(i), 0))
    scratch = [pltpu.VMEM((heads, rows + 2 * SUBLANES * nb, hd), F32), pltpu.VMEM((heads, rows, hd), F32),
               pltpu.VMEM((heads, rows, hd), F32)]
    if natural_out:
        in_specs.append(slab)
        args.append(h_prev)
        out0 = jax.ShapeDtypeStruct((nb, seq, r), F32)
        out_spec0 = pl.BlockSpec((nb, steps, r), lambda i: (0, chunk(i), 0))
        scratch.append(pltpu.VMEM((heads, rows, hd), F32))
    else:
        out0 = jax.ShapeDtypeStruct((heads, seq * nb, hd), F32)
        out_spec0 = slab
    blk = _nbytes((rows, r), F32)
    vmem = 12 * blk + 2 * _nbytes(wg.shape, wg.dtype)
    return pl.pallas_call(
        functools.partial(_lru_kernel, reverse=reverse, natural_out=natural_out, steps=steps),
        out_shape=[out0, jax.ShapeDtypeStruct((heads, nb, hd), F32)],
        grid=(n,),
        in_specs=in_specs,
        out_specs=[out_spec0, full((heads, nb, hd))],
        scratch_shapes=scratch,
        compiler_params=_params(("arbitrary",), vmem),
        name="lru_reverse" if reverse else "lru_forward",
    )(*args)


def _cmul(re, im, c, s):
    return re * c + im * s, im * c - re * s


def _fft_lists(re, im):
    n = len(re)
    if n == 1:
        return re, im
    er, ei = _fft_lists(re[0::2], im[0::2])
    qr, qi = _fft_lists(re[1::2], im[1::2])
    out_r, out_i = [None] * n, [None] * n
    for k in range(n // 2):
        ang = 2.0 * math.pi * k / n
        if k == 0:
            tr, ti = qr[k], qi[k]
        elif 4 * k == n:
            tr, ti = qi[k], -qr[k]
        else:
            tr, ti = _cmul(qr[k], qi[k], math.cos(ang), math.sin(ang))
        out_r[k], out_i[k] = er[k] + tr, ei[k] + ti
        out_r[k + n // 2], out_i[k + n // 2] = er[k] - tr, ei[k] - ti
    return out_r, out_i


def _dft_kernel(x_ref, w_ref, twc_ref, tws_ref, u_ref, v_ref, a_s):
    rdx = DFT_RADIX
    seq = x_ref.shape[1]
    m = seq // rdx
    w = w_ref[...].astype(BF16)
    for s1 in range(0, rdx, 2):
        xa = x_ref[0, pl.ds(s1, m, stride=rdx), :]
        xb = x_ref[0, pl.ds(s1 + 1, m, stride=rdx), :]
        xx = jnp.concatenate([xa, xb], axis=1).astype(BF16)
        a = jnp.dot(w, xx, preferred_element_type=F32)
        a_s[s1] = a[:, :LANES]
        a_s[s1 + 1] = a[:, LANES:]

    blk = 128

    def rows(t, carry):
        r0 = pl.multiple_of(t * blk, blk)
        re, im = [], []
        for s1 in range(rdx):
            ar = a_s[s1, pl.ds(r0, blk), :]
            ai = a_s[s1, pl.ds(m + r0, blk), :]
            if s1:
                ar, ai = _cmul(ar, ai, twc_ref[s1, pl.ds(r0, blk), :], tws_ref[s1, pl.ds(r0, blk), :])
            re.append(ar)
            im.append(ai)
        yr, yi = _fft_lists(re, im)
        for k1 in range(rdx):
            u_ref[0, pl.ds(k1 * m + r0, blk), :] = yr[k1].astype(u_ref.dtype)
            v_ref[0, pl.ds(k1 * m + r0, blk), :] = (-yi[k1]).astype(v_ref.dtype)
        return carry

    lax.fori_loop(0, m // blk, rows, 0)


def _position_dft(x):
    bsz, seq, c = x.shape
    rdx = DFT_RADIX
    m = seq // rdx
    k = np.arange(m)
    ang = 2.0 * np.pi * np.outer(k, k) / m
    w = jnp.asarray(np.concatenate([np.cos(ang), -np.sin(ang)], axis=0), F32)
    tw = 2.0 * np.pi * np.outer(np.arange(rdx), k) / seq
    bc = lambda t: jnp.asarray(np.broadcast_to(t[:, :, None], (rdx, m, LANES)), F32)
    tok = pl.BlockSpec((1, seq, LANES), lambda b, j: (b, 0, j))
    vmem = 2 * (3 * _nbytes((seq, LANES), F32) + _nbytes((2 * m, m), F32) + 2 * _nbytes((rdx, m, LANES), F32)) \
        + _nbytes((rdx, 2 * m, LANES), F32) + 16 * _nbytes((2 * m, LANES), F32)
    return pl.pallas_call(
        _dft_kernel,
        out_shape=[jax.ShapeDtypeStruct((bsz, seq, c), BF16)] * 2,
        grid=(bsz, c // LANES),
        in_specs=[
            tok,
            pl.BlockSpec((2 * m, m), lambda b, j: (0, 0)),
            pl.BlockSpec((rdx, m, LANES), lambda b, j: (0, 0, 0)),
            pl.BlockSpec((rdx, m, LANES), lambda b, j: (0, 0, 0)),
        ],
        out_specs=[tok, tok],
        scratch_shapes=[pltpu.VMEM((rdx, 2 * m, LANES), F32)],
        compiler_params=_params(("parallel", "parallel"), vmem),
        name="position_dft",
    )(x, w, bc(np.cos(tw)), bc(np.sin(tw)))


def _fold_kernel(cc_ref, sc_ref, w_ref, o_ref):
    w = w_ref[...]
    o_ref[0] = _dot3(cc_ref[...], w).astype(o_ref.dtype)
    o_ref[1] = (-_dot3(sc_ref[...], w)).astype(o_ref.dtype)


def _fold_channel_dft(w_fourier, seq):
    f, d = w_fourier.shape
    gd = f // FOURIER_GROUPS
    k = np.arange(gd)
    ang = 2.0 * np.pi * np.outer(k, k) / gd
    scale = 1.0 / math.sqrt(seq * gd)
    out = pl.pallas_call(
        _fold_kernel,
        out_shape=jax.ShapeDtypeStruct((2, f, d), BF16),
        grid=(FOURIER_GROUPS,),
        in_specs=[
            pl.BlockSpec((gd, gd), lambda g: (0, 0)),
            pl.BlockSpec((gd, gd), lambda g: (0, 0)),
            pl.BlockSpec((gd, d), lambda g: (g, 0)),
        ],
        out_specs=pl.BlockSpec((2, gd, d), lambda g: (0, g, 0)),
        compiler_params=_params(("parallel",), 32 * _nbytes((gd, d), F32)),
        name="fold_channel_dft",
    )(jnp.asarray(np.cos(ang) * scale, F32), jnp.asarray(np.sin(ang) * scale, F32), w_fourier)
    return out.reshape(2 * f, d)


def _merge_kernel(u_ref, v_ref, hs_ref, zy_ref, gf_ref, gr_ref, x_ref, g1_ref, sh2_ref, sc2_ref, n2_ref,
                  wcs_ref, wr_ref, wo_ref, x1_ref, h2_ref):
    uv = jnp.concatenate([u_ref[0], v_ref[0]], axis=1)
    f = jnp.dot(uv, wcs_ref[...], preferred_element_type=F32)
    p = (hs_ref[0] * _gelu_tanh(zy_ref[0].astype(F32))).astype(BF16)
    r = jnp.dot(p, wr_ref[...], preferred_element_type=F32)
    m = _sigmoid(gf_ref[0].astype(F32)) * f + _sigmoid(gr_ref[0].astype(F32)) * r
    o = jnp.dot(m.astype(BF16), wo_ref[...], preferred_element_type=F32)
    x1 = x_ref[0] + g1_ref[0] * o
    x1_ref[0] = x1
    h2_ref[0] = _rms_modulate(x1, n2_ref[...], sh2_ref[0], sc2_ref[0]).astype(h2_ref.dtype)


def _merge(u, v, hsum, zy, zg, x, mods, norm2_g, wcs, w_r, w_o, ts):
    bsz, seq, d = x.shape
    tok = lambda c, k=0: pl.BlockSpec((1, ts, c), lambda b, i: (b, i, k))
    mod = lambda k: pl.BlockSpec((1, 1, d), lambda b, i: (b, 0, k))
    full = lambda a: pl.BlockSpec(a.shape, lambda b, i: (0, 0))
    vmem = 2 * sum(_nbytes(a.shape, a.dtype) for a in (wcs, w_r, w_o)) + 24 * _nbytes((ts, d), F32)
    return pl.pallas_call(
        _merge_kernel,
        out_shape=[jax.ShapeDtypeStruct((bsz, seq, d), F32), jax.ShapeDtypeStruct((bsz, seq, d), BF16)],
        grid=(bsz, seq // ts),
        in_specs=[
            tok(u.shape[2]), tok(v.shape[2]), tok(hsum.shape[2]), tok(zy.shape[2]), tok(d, 0), tok(d, 1),
            tok(d), mod(2), mod(3), mod(4),
            pl.BlockSpec((1, d), lambda b, i: (0, 0)),
            full(wcs), full(w_r), full(w_o),
        ],
        out_specs=[tok(d), tok(d)],
        compiler_params=_params(("parallel", "parallel"), vmem),
        name="merge_mixers",
    )(u, v, hsum, zy, zg, zg, x, mods, mods, mods, norm2_g.reshape(1, d), wcs, w_r, w_o)


def _ffn_kernel(hp_ref, hm_ref, hn_ref, wg_ref, wv_ref, wgn_ref, wvn_ref, cwg_ref, cwv_ref, cbg_ref, cbv_ref,
                wd_ref, x1_ref, g2_ref, fg_ref, o_ref, ua_s, ub_s, p_s):
    part = pl.program_id(1)
    nparts = pl.num_programs(1)
    j = pl.program_id(2)
    nch = pl.num_programs(2)
    rows, d = hm_ref.shape[1], hm_ref.shape[2]
    pairs = wd_ref.shape[0] // LANES
    pad = SUBLANES + GRID_W
    tiles_per_row = GRID_W // SUBLANES
    nblk = rows // PIPE_ROWS
    gpb = PIPE_ROWS // GRID_W
    head = 2
    assert nblk >= head + 2
    sub = lax.broadcasted_iota(jnp.int32, (SUBLANES, LANES), 0)

    def w_pair(wrefs, k):
        ln = slice(k * LANES, (k + 1) * LANES)
        return jnp.concatenate([wrefs[0][:, ln], wrefs[1][:, ln]], axis=1)

    def up_halo(u_s, wrefs):
        for k in range(pairs):
            w = w_pair(wrefs, k)
            top = jnp.where(part > 0, jnp.dot(hp_ref[0], w, preferred_element_type=F32), 0.0)
            bot = jnp.where(part < nparts - 1, jnp.dot(hn_ref[0], w, preferred_element_type=F32), 0.0)
            for gv in range(2):
                u_s[gv * pairs + k, SUBLANES:pad, :] = top[:, gv * LANES:(gv + 1) * LANES]
                u_s[gv * pairs + k, pad + rows:pad + rows + GRID_W, :] = bot[:, gv * LANES:(gv + 1) * LANES]
        return [u_s[k, pad + rows:pad + rows + SUBLANES, :] for k in range(pairs)]

    def up_block(u_s, wrefs, t):
        r0 = t * PIPE_ROWS
        for k in range(pairs):
            u = jnp.dot(hm_ref[0, r0:r0 + PIPE_ROWS, :], w_pair(wrefs, k), preferred_element_type=F32)
            for gv in range(2):
                u_s[gv * pairs + k, pad + r0:pad + r0 + PIPE_ROWS, :] = u[:, gv * LANES:(gv + 1) * LANES]
        return [u_s[k, pad + r0:pad + r0 + SUBLANES, :] for k in range(pairs)]

    def down_block(t):
        r0 = t * PIPE_ROWS
        p = p_s[r0:r0 + PIPE_ROWS, :]
        for n0 in range(0, d, 2 * LANES):
            o_ref[0, r0:r0 + PIPE_ROWS, n0:n0 + 2 * LANES] += jnp.dot(p, wd_ref[:, n0:n0 + 2 * LANES],
                                                                         preferred_element_type=F32)
        return [o_ref[0, r0:r0 + SUBLANES, n0:n0 + LANES] for n0 in range(0, d, 2 * LANES)]

    def zero_after(witnesses):
        bits = pltpu.bitcast(witnesses[0], jnp.uint32)
        for w in witnesses[1:]:
            bits = bits | pltpu.bitcast(w, jnp.uint32)
        half = jnp.uint32(16)
        bits = lax.shift_right_logical(lax.shift_right_logical(bits, half), half)
        return pltpu.bitcast(bits, F32)

    def conv_block(u_s, t, zero):
        cw = (cwg_ref, cwv_ref)
        cb = (cbg_ref, cbv_ref)
        for k in range(pairs):
            ln = slice(k * LANES, (k + 1) * LANES)
            wt = [[[jnp.broadcast_to(cw[gv][3 * dh + dw:3 * dh + dw + 1, ln], (SUBLANES, LANES))
                    for dw in range(3)] for dh in range(3)] for gv in range(2)]
            bias = [jnp.broadcast_to(cb[gv][:, ln], (SUBLANES, LANES)) for gv in range(2)]
            if zero is not None:
                bias = [b + zero for b in bias]
            for g in range(t * gpb, (t + 1) * gpb):
                g0 = g * GRID_W
                gated = []
                for tile in range(tiles_per_row):
                    base = pad + g0 + tile * SUBLANES
                    acc = []
                    for gv in range(2):
                        cols = []
                        for dw in range(3):
                            col = None
                            for dh in range(3):
                                off = (dh - 1) * GRID_W + (dw - 1)
                                term = wt[gv][dh][dw] * u_s[gv * pairs + k, base + off:base + off + SUBLANES, :]
                                col = term if col is None else col + term
                            cols.append(col)
                        if tile == 0:
                            cols[0] = jnp.where(sub == 0, 0.0, cols[0])
                        if tile == tiles_per_row - 1:
                            cols[2] = jnp.where(sub == SUBLANES - 1, 0.0, cols[2])
                        acc.append((bias[gv] + cols[1]) + (cols[0] + cols[2]))
                    gated.append(_twice_gelu_tanh(acc[0]) * acc[1])
                p_s[g0:g0 + GRID_W, ln] = jnp.concatenate(gated, axis=0).astype(p_s.dtype)

    cur_w = (wg_ref, wv_ref)
    nxt_w = (wgn_ref, wvn_ref)

    @pl.when(j == 0)
    def _():
        zeros = jnp.zeros((SUBLANES, LANES), F32)
        for u_s in (ua_s, ub_s):
            for s in range(2 * pairs):
                u_s[s, 0:SUBLANES, :] = zeros
                u_s[s, pad + rows + GRID_W:pad + rows + GRID_W + SUBLANES, :] = zeros
        o_ref[...] = jnp.zeros(o_ref.shape, o_ref.dtype)
        up_halo(ua_s, cur_w)
        for t in range(head):
            up_block(ua_s, cur_w, t)

    def step(u_cur, u_nxt):
        ahead = [functools.partial(up_block, u_cur, cur_w, t) for t in range(head, nblk)]
        ahead += [functools.partial(up_halo, u_nxt, nxt_w)]
        ahead += [functools.partial(up_block, u_nxt, nxt_w, t) for t in range(head)]
        per_slot = -(-len(ahead) // (nblk - 1))
        zero = None
        for t in range(nblk):
            witnesses = []
            for fn in ahead[:per_slot]:
                witnesses += fn()
            ahead = ahead[per_slot:]
            if t >= 1:
                witnesses += down_block(t - 1)
            conv_block(u_cur, t, zero)
            zero = zero_after(witnesses) if witnesses else None
        assert not ahead
        down_block(nblk - 1)

    @pl.when(j % 2 == 0)
    def _():
        step(ua_s, ub_s)

    @pl.when(j % 2 == 1)
    def _():
        step(ub_s, ua_s)

    @pl.when(j == nch - 1)
    def _():
        blk = 256

        def finish(t, carry):
            r0 = pl.multiple_of(t * blk, blk)
            x2 = x1_ref[0, pl.ds(r0, blk), :] + g2_ref[0] * o_ref[0, pl.ds(r0, blk), :]
            ms = jnp.mean(x2 * x2, axis=-1, keepdims=True)
            o_ref[0, pl.ds(r0, blk), :] = x2 * lax.rsqrt(ms + RMS_EPS) * fg_ref[...]
            return carry

        lax.fori_loop(0, rows // blk, finish, 0)


def _conv_ffn_final(h2, x1, mods, final_g, w_up, conv_w, conv_b, w_down, *, ck):
    bsz, seq, d = h2.shape
    d_ff = w_down.shape[0]
    nch = d_ff // ck
    pairs = ck // LANES
    nparts = 2
    rows = seq // nparts
    gpr = rows // GRID_W
    cw = conv_w.reshape(9, 2 * d_ff)
    cb = conv_b.reshape(1, 2 * d_ff)
    u_rows = rows + 2 * (GRID_W + SUBLANES)
    nxt = lambda j: jnp.minimum(j + 1, nch - 1)
    gate = lambda shape, f=(lambda j: j): pl.BlockSpec(shape, lambda b, p, j: (0, f(j)))
    value = lambda shape, f=(lambda j: j): pl.BlockSpec(shape, lambda b, p, j: (0, nch + f(j)))
    img = pl.BlockSpec((1, rows, d), lambda b, p, j: (b, p, 0))
    img_once = pl.BlockSpec((1, rows, d), lambda b, p, j: (b, p, 0), pipeline_mode=pl.Buffered(1))
    vmem = (_nbytes((rows + 4 * GRID_W, d), BF16) + 4 * _nbytes((rows, d), F32)
            + 2 * _nbytes((2 * pairs, u_rows, LANES), F32) + _nbytes((rows, ck), BF16)
            + 8 * _nbytes((d, ck), BF16) + 2 * _nbytes((ck, d), BF16) + 16 * _nbytes((PIPE_ROWS, 2 * LANES), F32))
    return pl.pallas_call(
        _ffn_kernel,
        out_shape=jax.ShapeDtypeStruct((bsz, seq, d), F32),
        grid=(bsz, nparts, nch),
        in_specs=[
            pl.BlockSpec((1, GRID_W, d), lambda b, p, j: (b, jnp.maximum(p * gpr - 1, 0), 0)),
            img_once,
            pl.BlockSpec((1, GRID_W, d), lambda b, p, j: (b, jnp.minimum((p + 1) * gpr, seq // GRID_W - 1), 0)),
            gate((d, ck)), value((d, ck)), gate((d, ck), nxt), value((d, ck), nxt),
            gate((9, ck)), value((9, ck)), gate((1, ck)), value((1, ck)),
            pl.BlockSpec((ck, d), lambda b, p, j: (j, 0)),
            img,
            pl.BlockSpec((1, 1, d), lambda b, p, j: (b, 0, 5)),
            pl.BlockSpec((1, d), lambda b, p, j: (0, 0)),
        ],
        out_specs=img,
        scratch_shapes=[
            pltpu.VMEM((2 * pairs, u_rows, LANES), F32),
            pltpu.VMEM((2 * pairs, u_rows, LANES), F32),
            pltpu.VMEM((rows, ck), BF16),
        ],
        compiler_params=_params(("parallel", "parallel", "arbitrary"), vmem),
        name="conv_ffn",
    )(h2, h2, h2, w_up, w_up, w_up, w_up, cw, cw, cb, cb, w_down, x1, mods, final_g.reshape(1, d))


def kernel(x, c, ctx, c_ctx, mod_w, mod_b, norm1_g, norm2_g, w_in, lru_conv_w, lru_conv_b, lru_ga_w, lru_ga_b,
           lru_gx_w, lru_gx_b, lru_lambda, w_fourier, w_lru_out, w_o, ffn_w_up, ffn_conv_w, ffn_conv_b,
           ffn_w_down, final_g):
    bsz, seq, d = x.shape
    ctx_len = ctx.shape[1]
    assert mod_w.shape[0] == 1 and bsz == SUBLANES
    l = 0
    f = w_fourier.shape[1]
    r = w_lru_out.shape[1]
    heads, hd = lru_ga_w.shape[2], lru_ga_w.shape[3]

    cond = jnp.concatenate([c, c_ctx[None], jnp.zeros((SUBLANES - 1, d), F32)], axis=0)
    mods = _ada_params(cond, mod_w[l], mod_b[l]).reshape(cond.shape[0], 1, N_MOD * d)

    w_in_b = w_in[l].astype(BF16)
    w_f, w_x, w_y, w_g = (w_in_b[:, :f], w_in_b[:, f:f + r], w_in_b[:, f + r:f + 2 * r], w_in_b[:, f + 2 * r:])
    wg = (0.5 * jnp.concatenate([lru_ga_w[l], lru_gx_w[l]], axis=-1)).astype(BF16)
    lru_p = lambda dr: (lru_conv_w[l], lru_conv_b[l][None], wg[dr], 0.5 * lru_ga_b[l, dr][None],
                        0.5 * lru_gx_b[l, dr][None], lru_lambda[l, dr][None])

    (zc_x,) = _in_projection(ctx, mods, lambda b: bsz, norm1_g[l], [w_x], [F32], ts=ctx_len)
    zero = jnp.zeros((heads, bsz, hd), F32)
    _, h0_f = _lru_pass(zc_x, lru_p(0), zero, None, reverse=False, steps=64)
    _, h0_b = _lru_pass(zc_x, lru_p(1), zero, None, reverse=True, steps=64)

    z_f, z_x, z_y, z_g = _in_projection(x, mods, lambda b: b, norm1_g[l], [w_f, w_x, w_y, w_g],
                                        [F32, F32, BF16, BF16], ts=512)
    h_f, _ = _lru_pass(z_x, lru_p(0), h0_f, None, reverse=False, steps=64)
    h_sum, _ = _lru_pass(z_x, lru_p(1), h0_b, h_f, reverse=True, steps=64)
    u, v = _position_dft(z_f)
    wcs = _fold_channel_dft(w_fourier[l], seq)
    x1, h2 = _merge(u, v, h_sum, z_y, z_g, x, mods, norm2_g[l], wcs,
                    w_lru_out[l].astype(BF16), w_o[l].astype(BF16), ts=512)
    return _conv_ffn_final(h2, x1, mods, final_g, ffn_w_up[l].astype(BF16), ffn_conv_w[l], ffn_conv_b[l],
                           (0.5 * ffn_w_down[l]).astype(BF16), ck=256)
```

```python
import functools
import math

import numpy as np
import jax
import jax.numpy as jnp
from jax import lax
from jax.experimental import pallas as pl
from jax.experimental.pallas import tpu as pltpu

F32 = jnp.float32
BF16 = jnp.bfloat16

GRID_W = 64
FOURIER_GROUPS = 4
LRU_CONV_W = 4
LRU_C = 8.0
N_MOD = 6
RMS_EPS = 1e-6
LN_2 = math.log(2.0)
LOG2_E = 1.0 / LN_2
DFT_RADIX = 8

V7X_VMEM_BYTES = 64 * 1024 * 1024
SUBLANES = 8
LANES = 128
PIPE_ROWS = 512


def _params(semantics, vmem_bytes):
    assert vmem_bytes < V7X_VMEM_BYTES, vmem_bytes
    return pltpu.CompilerParams(dimension_semantics=semantics, vmem_limit_bytes=int(vmem_bytes))


def _nbytes(shape, dtype):
    return math.prod(shape) * jnp.dtype(dtype).itemsize


def _sigmoid(x):
    return 0.5 * jnp.tanh(0.5 * x) + 0.5


def _gelu_tanh(x):
    c = math.sqrt(2.0 / math.pi)
    return 0.5 * x * (1.0 + jnp.tanh(c * (x + 0.044715 * (x * x * x))))


def _twice_gelu_tanh(x):
    c = math.sqrt(2.0 / math.pi)
    t = jnp.tanh(x * (c + (c * 0.044715) * (x * x)))
    return x + x * t


def _split_bf16(x):
    hi = x.astype(BF16)
    lo = (x - hi.astype(F32)).astype(BF16)
    return hi, lo


def _dot3(a, b):
    a_hi, a_lo = _split_bf16(a)
    b_hi, b_lo = _split_bf16(b)
    d = functools.partial(jnp.dot, preferred_element_type=F32)
    return d(a_hi, b_hi) + (d(a_hi, b_lo) + d(a_lo, b_hi))


def _rms_modulate(x, g, shift, scale):
    ms = jnp.mean(x * x, axis=-1, keepdims=True)
    y = x * lax.rsqrt(ms + RMS_EPS) * g
    return y * (1.0 + scale) + shift


def _mod_kernel(c_ref, w_ref, b_ref, o_ref):
    c = c_ref[...]
    s = c * _sigmoid(c)
    o_ref[...] = _dot3(s, w_ref[...]) + b_ref[...]


def _ada_params(cond, w, b):
    n, d = cond.shape
    width = w.shape[1]
    tn = d
    return pl.pallas_call(
        _mod_kernel,
        out_shape=jax.ShapeDtypeStruct((n, width), F32),
        grid=(width // tn,),
        in_specs=[
            pl.BlockSpec((n, d), lambda j: (0, 0)),
            pl.BlockSpec((d, tn), lambda j: (0, j)),
            pl.BlockSpec((1, tn), lambda j: (0, j)),
        ],
        out_specs=pl.BlockSpec((n, tn), lambda j: (0, j)),
        compiler_params=_params(("parallel",), 6 * _nbytes((d, tn), F32)),
        name="ada_params",
    )(cond, w, b.reshape(1, width))


def _inproj_kernel(x_ref, sh_ref, sc_ref, g_ref, *refs, slab_out):
    n_out = len(refs) // 2
    nb, tp, d = x_ref.shape
    h = _rms_modulate(x_ref[...], g_ref[...], sh_ref[...], sc_ref[...]).reshape(nb * tp, d).astype(BF16)
    for idx, (w_ref, o_ref) in enumerate(zip(refs[:n_out], refs[n_out:])):
        z = jnp.dot(h, w_ref[...], preferred_element_type=F32)
        if idx == slab_out:
            for s in range(o_ref.shape[0]):
                for b in range(nb):
                    o_ref[s, pl.ds(b, tp, stride=nb), :] = z[b * tp:(b + 1) * tp, s * LANES:(s + 1) * LANES]
        else:
            o_ref[...] = z.reshape(nb, tp, z.shape[1]).astype(o_ref.dtype)


def _in_projection(x, mods, mod_rows, norm_g, weights, out_dtypes, slab_out, tp):
    bsz, seq, d = x.shape
    assert bsz == SUBLANES and tp % SUBLANES == 0
    outs, out_specs = [], []
    for idx, (w, dt) in enumerate(zip(weights, out_dtypes)):
        c = w.shape[1]
        if idx == slab_out:
            outs.append(jax.ShapeDtypeStruct((c // LANES, seq * bsz, LANES), dt))
            out_specs.append(pl.BlockSpec((c // LANES, tp * bsz, LANES), lambda i: (0, i, 0)))
        else:
            outs.append(jax.ShapeDtypeStruct((bsz, seq, c), dt))
            out_specs.append(pl.BlockSpec((bsz, tp, c), lambda i: (0, i, 0)))
    w_specs = [pl.BlockSpec(w.shape, lambda i: (0, 0)) for w in weights]
    first, count = mod_rows
    assert first % count == 0
    mod = lambda k: pl.BlockSpec((count, 1, d), lambda i: (first // count, 0, k))
    rows = bsz * tp
    vmem = 2 * (_nbytes((rows, d), F32) + sum(_nbytes(w.shape, w.dtype) for w in weights)
                + sum(_nbytes((rows, w.shape[1]), F32) for w in weights)) + 4 * _nbytes((rows, d), F32)
    return pl.pallas_call(
        functools.partial(_inproj_kernel, slab_out=slab_out),
        out_shape=outs,
        grid=(seq // tp,),
        in_specs=[
            pl.BlockSpec((bsz, tp, d), lambda i: (0, i, 0)),
            mod(0),
            mod(1),
            pl.BlockSpec((1, d), lambda i: (0, 0)),
        ] + w_specs,
        out_specs=out_specs,
        compiler_params=_params(("parallel",), vmem),
        name="in_projection",
    )(x, mods, mods, norm_g.reshape(1, d), *weights)


def _lru_kernel(*refs, reverse, natural_out, steps):
    if natural_out:
        (zp_ref, zm_ref, zn_ref, cw_ref, cb_ref, wg_ref, gab_ref, gxb_ref, lam_ref, h0_ref, hprev_ref,
         o_ref, hl_ref, a_s, b_s, hbuf) = refs
    else:
        (zp_ref, zm_ref, zn_ref, cw_ref, cb_ref, wg_ref, gab_ref, gxb_ref, lam_ref, h0_ref,
         o_ref, hl_ref, a_s, b_s) = refs
    i = pl.program_id(0)
    n = pl.num_programs(0)
    ci = n - 1 - i if reverse else i
    nb = h0_ref.shape[1]
    heads, hd = wg_ref.shape[0], wg_ref.shape[1]
    rows = steps * nb
    halo = SUBLANES * nb

    @pl.when(i == 0)
    def _():
        hl_ref[...] = h0_ref[...]

    left = LRU_CONV_W // 2

    def coeffs(h, carry):
        z = jnp.concatenate([jnp.where(ci > 0, zp_ref[h], 0.0), zm_ref[h],
                             jnp.where(ci < n - 1, zn_ref[h], 0.0)], axis=0)
        u = cb_ref[h]
        for k in range(LRU_CONV_W):
            start = halo + (k - left) * nb
            u = u + cw_ref[h, k:k + 1, :] * z[start:start + rows]
        g = jnp.dot(u.astype(BF16), wg_ref[h], preferred_element_type=F32)
        tr = jnp.tanh(g[:, :hd] + gab_ref[h])
        gi = 0.5 * jnp.tanh(g[:, hd:] + gxb_ref[h]) + 0.5
        nl = -lam_ref[h]
        softplus = jnp.maximum(nl, 0.0) + jnp.log1p(jnp.exp(-jnp.abs(nl)))
        half = (-0.5 * LRU_C * LOG2_E) * softplus
        log2_a = tr * half + half
        a = jnp.exp2(log2_a)
        m2 = jnp.maximum(jnp.tanh(log2_a * (-LN_2)) * (a * a + 1.0), 1e-12)
        mult = m2 * lax.rsqrt(m2)
        a_s[h] = a
        b_s[h] = mult * (gi * u)
        return carry

    lax.fori_loop(0, heads, coeffs, 0)

    def emit(h, r0, v):
        if natural_out:
            hbuf[h, r0:r0 + nb, :] = v + hprev_ref[h, r0:r0 + nb, :]
        else:
            o_ref[h, r0:r0 + nb, :] = v

    hs = [hl_ref[h] for h in range(heads)]
    for i in range(steps // 2):
        t0 = steps - 1 - 2 * i if reverse else 2 * i
        t1 = t0 - 1 if reverse else t0 + 1
        r0, r1 = t0 * nb, t1 * nb
        for h in range(heads):
            a0, b0 = a_s[h, r0:r0 + nb, :], b_s[h, r0:r0 + nb, :]
            a1, b1 = a_s[h, r1:r1 + nb, :], b_s[h, r1:r1 + nb, :]
            h1 = (a1 * a0) * hs[h] + (a1 * b0 + b1)
            emit(h, r0, a0 * hs[h] + b0)
            emit(h, r1, h1)
            hs[h] = h1
    for h in range(heads):
        hl_ref[h] = hs[h]

    if natural_out:
        for h in range(heads):
            for b in range(nb):
                o_ref[b, :, h * hd:(h + 1) * hd] = hbuf[h, pl.ds(b, steps, stride=nb), :]


def _lru_pass(zx, lru_p, h0, h_prev, *, reverse, steps):
    conv_w, conv_b, wg, ga_b, gx_b, lam = lru_p
    heads, total, hd = zx.shape
    nb = h0.shape[1]
    seq, r = total // nb, heads * hd
    assert nb == SUBLANES and hd == LANES and wg.shape[:2] == (heads, hd)
    assert seq % steps == 0 and steps % SUBLANES == 0
    rows = steps * nb
    halo = SUBLANES * nb
    n = seq // steps
    natural_out = h_prev is not None
    chunk = (lambda i: n - 1 - i) if reverse else (lambda i: i)
    per_head = lambda a: a.reshape(-1, heads, hd).transpose(1, 0, 2)
    full = lambda shape: pl.BlockSpec(shape, lambda i: (0,) * len(shape))
    hpc = rows // halo
    in_specs = [
        pl.BlockSpec((heads, halo, hd), lambda i: (0, jnp.maximum(chunk(i) * hpc - 1, 0), 0)),
        pl.BlockSpec((heads, rows, hd), lambda i: (0, chunk(i), 0)),
        pl.BlockSpec((heads, halo, hd), lambda i: (0, jnp.minimum((chunk(i) + 1) * hpc, total // halo - 1), 0)),
        full((heads, LRU_CONV_W, hd)), full((heads, 1, hd)), full(wg.shape), full((heads, 1, hd)),
        full((heads, 1, hd)), full((heads, 1, hd)), full((heads, nb, hd)),
    ]
    args = [zx, zx, zx, per_head(conv_w), per_head(conv_b), wg, per_head(ga_b), per_head(gx_b), per_head(lam), h0]
    slab = pl.BlockSpec((heads, rows, hd), lambda i: (0, chunk(i), 0))
    scratch = [pltpu.VMEM((heads, rows, hd), F32), pltpu.VMEM((heads, rows, hd), F32)]
    if natural_out:
        in_specs.append(slab)
        args.append(h_prev)
        out0 = jax.ShapeDtypeStruct((nb, seq, r), F32)
        out_spec0 = pl.BlockSpec((nb, steps, r), lambda i: (0, chunk(i), 0))
        scratch.append(pltpu.VMEM((heads, rows, hd), F32))
    else:
        out0 = jax.ShapeDtypeStruct((heads, seq * nb, hd), F32)
        out_spec0 = slab
    blk = _nbytes((rows, r), F32)
    vmem = 12 * blk + 2 * _nbytes(wg.shape, wg.dtype)
    return pl.pallas_call(
        functools.partial(_lru_kernel, reverse=reverse, natural_out=natural_out, steps=steps),
        out_shape=[out0, jax.ShapeDtypeStruct((heads, nb, hd), F32)],
        grid=(n,),
        in_specs=in_specs,
        out_specs=[out_spec0, full((heads, nb, hd))],
        scratch_shapes=scratch,
        compiler_params=_params(("arbitrary",), vmem),
        name="lru_reverse" if reverse else "lru_forward",
    )(*args)


def _cmul(re, im, c, s):
    return re * c + im * s, im * c - re * s


def _fft_lists(re, im):
    n = len(re)
    if n == 1:
        return re, im
    er, ei = _fft_lists(re[0::2], im[0::2])
    qr, qi = _fft_lists(re[1::2], im[1::2])
    out_r, out_i = [None] * n, [None] * n
    for k in range(n // 2):
        ang = 2.0 * math.pi * k / n
        if k == 0:
            tr, ti = qr[k], qi[k]
        elif 4 * k == n:
            tr, ti = qi[k], -qr[k]
        else:
            tr, ti = _cmul(qr[k], qi[k], math.cos(ang), math.sin(ang))
        out_r[k], out_i[k] = er[k] + tr, ei[k] + ti
        out_r[k + n // 2], out_i[k + n // 2] = er[k] - tr, ei[k] - ti
    return out_r, out_i


def _dft_kernel(x_ref, w_ref, twc_ref, tws_ref, u_ref, v_ref, a_s):
    rdx = DFT_RADIX
    seq = x_ref.shape[1]
    m = seq // rdx
    w = w_ref[...].astype(BF16)
    for s1 in range(0, rdx, 2):
        xa = x_ref[0, pl.ds(s1, m, stride=rdx), :]
        xb = x_ref[0, pl.ds(s1 + 1, m, stride=rdx), :]
        xx = jnp.concatenate([xa, xb], axis=1).astype(BF16)
        a = jnp.dot(w, xx, preferred_element_type=F32)
        a_s[s1] = a[:, :LANES]
        a_s[s1 + 1] = a[:, LANES:]

    blk = 128

    def rows(t, carry):
        r0 = pl.multiple_of(t * blk, blk)
        re, im = [], []
        for s1 in range(rdx):
            ar = a_s[s1, pl.ds(r0, blk), :]
            ai = a_s[s1, pl.ds(m + r0, blk), :]
            if s1:
                ar, ai = _cmul(ar, ai, twc_ref[s1, pl.ds(r0, blk), :], tws_ref[s1, pl.ds(r0, blk), :])
            re.append(ar)
            im.append(ai)
        yr, yi = _fft_lists(re, im)
        for k1 in range(rdx):
            u_ref[0, pl.ds(k1 * m + r0, blk), :] = yr[k1].astype(u_ref.dtype)
            v_ref[0, pl.ds(k1 * m + r0, blk), :] = (-yi[k1]).astype(v_ref.dtype)
        return carry

    lax.fori_loop(0, m // blk, rows, 0)


def _position_dft(x):
    bsz, seq, c = x.shape
    rdx = DFT_RADIX
    m = seq // rdx
    k = np.arange(m)
    ang = 2.0 * np.pi * np.outer(k, k) / m
    w = jnp.asarray(np.concatenate([np.cos(ang), -np.sin(ang)], axis=0), F32)
    tw = 2.0 * np.pi * np.outer(np.arange(rdx), k) / seq
    bc = lambda t: jnp.asarray(np.broadcast_to(t[:, :, None], (rdx, m, LANES)), F32)
    tok = pl.BlockSpec((1, seq, LANES), lambda b, j: (b, 0, j))
    vmem = 2 * (3 * _nbytes((seq, LANES), F32) + _nbytes((2 * m, m), F32) + 2 * _nbytes((rdx, m, LANES), F32)) \
        + _nbytes((rdx, 2 * m, LANES), F32) + 16 * _nbytes((2 * m, LANES), F32)
    return pl.pallas_call(
        _dft_kernel,
        out_shape=[jax.ShapeDtypeStruct((bsz, seq, c), BF16)] * 2,
        grid=(bsz, c // LANES),
        in_specs=[
            tok,
            pl.BlockSpec((2 * m, m), lambda b, j: (0, 0)),
            pl.BlockSpec((rdx, m, LANES), lambda b, j: (0, 0, 0)),
            pl.BlockSpec((rdx, m, LANES), lambda b, j: (0, 0, 0)),
        ],
        out_specs=[tok, tok],
        scratch_shapes=[pltpu.VMEM((rdx, 2 * m, LANES), F32)],
        compiler_params=_params(("parallel", "parallel"), vmem),
        name="position_dft",
    )(x, w, bc(np.cos(tw)), bc(np.sin(tw)))


def _fold_kernel(cc_ref, sc_ref, w_ref, o_ref):
    w = w_ref[...]
    o_ref[0] = _dot3(cc_ref[...], w).astype(o_ref.dtype)
    o_ref[1] = (-_dot3(sc_ref[...], w)).astype(o_ref.dtype)


def _fold_channel_dft(w_fourier, seq):
    f, d = w_fourier.shape
    gd = f // FOURIER_GROUPS
    k = np.arange(gd)
    ang = 2.0 * np.pi * np.outer(k, k) / gd
    scale = 1.0 / math.sqrt(seq * gd)
    out = pl.pallas_call(
        _fold_kernel,
        out_shape=jax.ShapeDtypeStruct((2, f, d), BF16),
        grid=(FOURIER_GROUPS,),
        in_specs=[
            pl.BlockSpec((gd, gd), lambda g: (0, 0)),
            pl.BlockSpec((gd, gd), lambda g: (0, 0)),
            pl.BlockSpec((gd, d), lambda g: (g, 0)),
        ],
        out_specs=pl.BlockSpec((2, gd, d), lambda g: (0, g, 0)),
        compiler_params=_params(("parallel",), 32 * _nbytes((gd, d), F32)),
        name="fold_channel_dft",
    )(jnp.asarray(np.cos(ang) * scale, F32), jnp.asarray(np.sin(ang) * scale, F32), w_fourier)
    return out.reshape(2 * f, d)


def _merge_kernel(u_ref, v_ref, hs_ref, zy_ref, gf_ref, gr_ref, x_ref, g1_ref, sh2_ref, sc2_ref, n2_ref,
                  wcs_ref, wr_ref, wo_ref, x1_ref, h2_ref):
    uv = jnp.concatenate([u_ref[0], v_ref[0]], axis=1)
    f = jnp.dot(uv, wcs_ref[...], preferred_element_type=F32)
    p = (hs_ref[0] * _gelu_tanh(zy_ref[0].astype(F32))).astype(BF16)
    r = jnp.dot(p, wr_ref[...], preferred_element_type=F32)
    m = _sigmoid(gf_ref[0].astype(F32)) * f + _sigmoid(gr_ref[0].astype(F32)) * r
    o = jnp.dot(m.astype(BF16), wo_ref[...], preferred_element_type=F32)
    x1 = x_ref[0] + g1_ref[0] * o
    x1_ref[0] = x1
    h2_ref[0] = _rms_modulate(x1, n2_ref[...], sh2_ref[0], sc2_ref[0]).astype(h2_ref.dtype)


def _merge(u, v, hsum, zy, zg, x, mods, norm2_g, wcs, w_r, w_o, ts):
    bsz, seq, d = x.shape
    tok = lambda c, k=0: pl.BlockSpec((1, ts, c), lambda b, i: (b, i, k))
    mod = lambda k: pl.BlockSpec((1, 1, d), lambda b, i: (b, 0, k))
    full = lambda a: pl.BlockSpec(a.shape, lambda b, i: (0, 0))
    vmem = 2 * sum(_nbytes(a.shape, a.dtype) for a in (wcs, w_r, w_o)) + 24 * _nbytes((ts, d), F32)
    return pl.pallas_call(
        _merge_kernel,
        out_shape=[jax.ShapeDtypeStruct((bsz, seq, d), F32), jax.ShapeDtypeStruct((bsz, seq, d), BF16)],
        grid=(bsz, seq // ts),
        in_specs=[
            tok(u.shape[2]), tok(v.shape[2]), tok(hsum.shape[2]), tok(zy.shape[2]), tok(d, 0), tok(d, 1),
            tok(d), mod(2), mod(3), mod(4),
            pl.BlockSpec((1, d), lambda b, i: (0, 0)),
            full(wcs), full(w_r), full(w_o),
        ],
        out_specs=[tok(d), tok(d)],
        compiler_params=_params(("parallel", "parallel"), vmem),
        name="merge_mixers",
    )(u, v, hsum, zy, zg, zg, x, mods, mods, mods, norm2_g.reshape(1, d), wcs, w_r, w_o)


def _ffn_kernel(hp_ref, hm_ref, hn_ref, wg_ref, wv_ref, wgn_ref, wvn_ref, cwg_ref, cwv_ref, cbg_ref, cbv_ref,
                wd_ref, x1_ref, g2_ref, fg_ref, o_ref, ua_s, ub_s, p_s):
    part = pl.program_id(1)
    nparts = pl.num_programs(1)
    j = pl.program_id(2)
    nch = pl.num_programs(2)
    rows, d = hm_ref.shape[1], hm_ref.shape[2]
    pairs = wd_ref.shape[0] // LANES
    pad = SUBLANES + GRID_W
    tiles_per_row = GRID_W // SUBLANES
    nblk = rows // PIPE_ROWS
    gpb = PIPE_ROWS // GRID_W
    head = 2
    assert nblk >= head + 2
    sub = lax.broadcasted_iota(jnp.int32, (SUBLANES, LANES), 0)

    def w_pair(wrefs, k):
        ln = slice(k * LANES, (k + 1) * LANES)
        return jnp.concatenate([wrefs[0][:, ln], wrefs[1][:, ln]], axis=1)

    def up_halo(u_s, wrefs):
        for k in range(pairs):
            w = w_pair(wrefs, k)
            top = jnp.where(part > 0, jnp.dot(hp_ref[0], w, preferred_element_type=F32), 0.0)
            bot = jnp.where(part < nparts - 1, jnp.dot(hn_ref[0], w, preferred_element_type=F32), 0.0)
            for gv in range(2):
                u_s[gv * pairs + k, SUBLANES:pad, :] = top[:, gv * LANES:(gv + 1) * LANES]
                u_s[gv * pairs + k, pad + rows:pad + rows + GRID_W, :] = bot[:, gv * LANES:(gv + 1) * LANES]
        return [u_s[k, pad + rows:pad + rows + SUBLANES, :] for k in range(pairs)]

    def up_block(u_s, wrefs, t):
        r0 = t * PIPE_ROWS
        for k in range(pairs):
            u = jnp.dot(hm_ref[0, r0:r0 + PIPE_ROWS, :], w_pair(wrefs, k), preferred_element_type=F32)
            for gv in range(2):
                u_s[gv * pairs + k, pad + r0:pad + r0 + PIPE_ROWS, :] = u[:, gv * LANES:(gv + 1) * LANES]
        return [u_s[k, pad + r0:pad + r0 + SUBLANES, :] for k in range(pairs)]

    def down_block(t):
        r0 = t * PIPE_ROWS
        p = p_s[r0:r0 + PIPE_ROWS, :]
        for n0 in range(0, d, 2 * LANES):
            o_ref[0, r0:r0 + PIPE_ROWS, n0:n0 + 2 * LANES] += jnp.dot(p, wd_ref[:, n0:n0 + 2 * LANES],
                                                                         preferred_element_type=F32)
        return [o_ref[0, r0:r0 + SUBLANES, n0:n0 + LANES] for n0 in range(0, d, 2 * LANES)]

    def zero_after(witnesses):
        bits = pltpu.bitcast(witnesses[0], jnp.uint32)
        for w in witnesses[1:]:
            bits = bits | pltpu.bitcast(w, jnp.uint32)
        half = jnp.uint32(16)
        bits = lax.shift_right_logical(lax.shift_right_logical(bits, half), half)
        return pltpu.bitcast(bits, F32)

    def conv_block(u_s, t, zero):
        cw = (cwg_ref, cwv_ref)
        cb = (cbg_ref, cbv_ref)
        for k in range(pairs):
            ln = slice(k * LANES, (k + 1) * LANES)
            wt = [[[jnp.broadcast_to(cw[gv][3 * dh + dw:3 * dh + dw + 1, ln], (SUBLANES, LANES))
                    for dw in range(3)] for dh in range(3)] for gv in range(2)]
            bias = [jnp.broadcast_to(cb[gv][:, ln], (SUBLANES, LANES)) for gv in range(2)]
            if zero is not None:
                bias = [b + zero for b in bias]
            for g in range(t * gpb, (t + 1) * gpb):
                g0 = g * GRID_W
                gated = []
                for tile in range(tiles_per_row):
                    base = pad + g0 + tile * SUBLANES
                    acc = []
                    for gv in range(2):
                        cols = []
                        for dw in range(3):
                            col = None
                            for dh in range(3):
                                off = (dh - 1) * GRID_W + (dw - 1)
                                term = wt[gv][dh][dw] * u_s[gv * pairs + k, base + off:base + off + SUBLANES, :]
                                col = term if col is None else col + term
                            cols.append(col)
                        if tile == 0:
                            cols[0] = jnp.where(sub == 0, 0.0, cols[0])
                        if tile == tiles_per_row - 1:
                            cols[2] = jnp.where(sub == SUBLANES - 1, 0.0, cols[2])
                        acc.append((bias[gv] + cols[1]) + (cols[0] + cols[2]))
                    gated.append(_twice_gelu_tanh(acc[0]) * acc[1])
                p_s[g0:g0 + GRID_W, ln] = jnp.concatenate(gated, axis=0).astype(p_s.dtype)

    cur_w = (wg_ref, wv_ref)
    nxt_w = (wgn_ref, wvn_ref)

    @pl.when(j == 0)
    def _():
        zeros = jnp.zeros((SUBLANES, LANES), F32)
        for u_s in (ua_s, ub_s):
            for s in range(2 * pairs):
                u_s[s, 0:SUBLANES, :] = zeros
                u_s[s, pad + rows + GRID_W:pad + rows + GRID_W + SUBLANES, :] = zeros
        o_ref[...] = jnp.zeros(o_ref.shape, o_ref.dtype)
        up_halo(ua_s, cur_w)
        for t in range(head):
            up_block(ua_s, cur_w, t)

    def step(u_cur, u_nxt):
        ahead = [functools.partial(up_block, u_cur, cur_w, t) for t in range(head, nblk)]
        ahead += [functools.partial(up_halo, u_nxt, nxt_w)]
        ahead += [functools.partial(up_block, u_nxt, nxt_w, t) for t in range(head)]
        per_slot = -(-len(ahead) // (nblk - 1))
        zero = None
        for t in range(nblk):
            witnesses = []
            for fn in ahead[:per_slot]:
                witnesses += fn()
            ahead = ahead[per_slot:]
            if t >= 1:
                witnesses += down_block(t - 1)
            conv_block(u_cur, t, zero)
            zero = zero_after(witnesses) if witnesses else None
        assert not ahead
        down_block(nblk - 1)

    @pl.when(j % 2 == 0)
    def _():
        step(ua_s, ub_s)

    @pl.when(j % 2 == 1)
    def _():
        step(ub_s, ua_s)

    @pl.when(j == nch - 1)
    def _():
        blk = 256

        def finish(t, carry):
            r0 = pl.multiple_of(t * blk, blk)
            x2 = x1_ref[0, pl.ds(r0, blk), :] + g2_ref[0] * o_ref[0, pl.ds(r0, blk), :]
            ms = jnp.mean(x2 * x2, axis=-1, keepdims=True)
            o_ref[0, pl.ds(r0, blk), :] = x2 * lax.rsqrt(ms + RMS_EPS) * fg_ref[...]
            return carry

        lax.fori_loop(0, rows // blk, finish, 0)


def _conv_ffn_final(h2, x1, mods, final_g, w_up, conv_w, conv_b, w_down, *, ck):
    bsz, seq, d = h2.shape
    d_ff = w_down.shape[0]
    nch = d_ff // ck
    pairs = ck // LANES
    nparts = 2
    rows = seq // nparts
    gpr = rows // GRID_W
    cw = conv_w.reshape(9, 2 * d_ff)
    cb = conv_b.reshape(1, 2 * d_ff)
    u_rows = rows + 2 * (GRID_W + SUBLANES)
    nxt = lambda j: jnp.minimum(j + 1, nch - 1)
    gate = lambda shape, f=(lambda j: j): pl.BlockSpec(shape, lambda b, p, j: (0, f(j)))
    value = lambda shape, f=(lambda j: j): pl.BlockSpec(shape, lambda b, p, j: (0, nch + f(j)))
    img = pl.BlockSpec((1, rows, d), lambda b, p, j: (b, p, 0))
    img_once = pl.BlockSpec((1, rows, d), lambda b, p, j: (b, p, 0), pipeline_mode=pl.Buffered(1))
    vmem = (_nbytes((rows + 4 * GRID_W, d), BF16) + 4 * _nbytes((rows, d), F32)
            + 2 * _nbytes((2 * pairs, u_rows, LANES), F32) + _nbytes((rows, ck), BF16)
            + 8 * _nbytes((d, ck), BF16) + 2 * _nbytes((ck, d), BF16) + 16 * _nbytes((PIPE_ROWS, 2 * LANES), F32))
    return pl.pallas_call(
        _ffn_kernel,
        out_shape=jax.ShapeDtypeStruct((bsz, seq, d), F32),
        grid=(bsz, nparts, nch),
        in_specs=[
            pl.BlockSpec((1, GRID_W, d), lambda b, p, j: (b, jnp.maximum(p * gpr - 1, 0), 0)),
            img_once,
            pl.BlockSpec((1, GRID_W, d), lambda b, p, j: (b, jnp.minimum((p + 1) * gpr, seq // GRID_W - 1), 0)),
            gate((d, ck)), value((d, ck)), gate((d, ck), nxt), value((d, ck), nxt),
            gate((9, ck)), value((9, ck)), gate((1, ck)), value((1, ck)),
            pl.BlockSpec((ck, d), lambda b, p, j: (j, 0)),
            img,
            pl.BlockSpec((1, 1, d), lambda b, p, j: (b, 0, 5)),
            pl.BlockSpec((1, d), lambda b, p, j: (0, 0)),
        ],
        out_specs=img,
        scratch_shapes=[
            pltpu.VMEM((2 * pairs, u_rows, LANES), F32),
            pltpu.VMEM((2 * pairs, u_rows, LANES), F32),
            pltpu.VMEM((rows, ck), BF16),
        ],
        compiler_params=_params(("parallel", "parallel", "arbitrary"), vmem),
        name="conv_ffn",
    )(h2, h2, h2, w_up, w_up, w_up, w_up, cw, cw, cb, cb, w_down, x1, mods, final_g.reshape(1, d))


def kernel(x, c, ctx, c_ctx, mod_w, mod_b, norm1_g, norm2_g, w_in, lru_conv_w, lru_conv_b, lru_ga_w, lru_ga_b,
           lru_gx_w, lru_gx_b, lru_lambda, w_fourier, w_lru_out, w_o, ffn_w_up, ffn_conv_w, ffn_conv_b,
           ffn_w_down, final_g):
    bsz, seq, d = x.shape
    assert mod_w.shape[0] == 1 and bsz == SUBLANES
    l = 0
    f = w_fourier.shape[1]
    r = w_lru_out.shape[1]
    heads, hd = lru_ga_w.shape[2], lru_ga_w.shape[3]

    cond = jnp.concatenate([c, c_ctx[None], jnp.zeros((SUBLANES - 1, d), F32)], axis=0)
    mods = _ada_params(cond, mod_w[l], mod_b[l]).reshape(cond.shape[0], 1, N_MOD * d)

    w_in_b = w_in[l].astype(BF16)
    w_f, w_x, w_y, w_g = (w_in_b[:, :f], w_in_b[:, f:f + r], w_in_b[:, f + r:f + 2 * r], w_in_b[:, f + 2 * r:])
    wg = (0.5 * jnp.concatenate([lru_ga_w[l], lru_gx_w[l]], axis=-1)).astype(BF16)
    lru_p = lambda dr: (lru_conv_w[l], lru_conv_b[l][None], wg[dr], 0.5 * lru_ga_b[l, dr][None],
                        0.5 * lru_gx_b[l, dr][None], lru_lambda[l, dr][None])

    (zc_x,) = _in_projection(ctx, mods, (bsz, 1), norm1_g[l], [w_x], [F32], slab_out=0, tp=64)
    zero = jnp.zeros((heads, bsz, hd), F32)
    _, h0_f = _lru_pass(zc_x, lru_p(0), zero, None, reverse=False, steps=64)
    _, h0_b = _lru_pass(zc_x, lru_p(1), zero, None, reverse=True, steps=64)

    z_f, z_x, z_y, z_g = _in_projection(x, mods, (0, bsz), norm1_g[l], [w_f, w_x, w_y, w_g],
                                        [F32, F32, BF16, BF16], slab_out=1, tp=64)
    h_f, _ = _lru_pass(z_x, lru_p(0), h0_f, None, reverse=False, steps=128)
    h_sum, _ = _lru_pass(z_x, lru_p(1), h0_b, h_f, reverse=True, steps=128)
    u, v = _position_dft(z_f)
    wcs = _fold_channel_dft(w_fourier[l], seq)
    x1, h2 = _merge(u, v, h_sum, z_y, z_g, x, mods, norm2_g[l], wcs,
                    w_lru_out[l].astype(BF16), w_o[l].astype(BF16), ts=512)
    return _conv_ffn_final(h2, x1, mods, final_g, ffn_w_up[l].astype(BF16), ffn_conv_w[l], ffn_conv_b[l],
                           (0.5 * ffn_w_down[l]).astype(BF16), ck=256)
```

```python
import functools
import math

import numpy as np
import jax
import jax.numpy as jnp
from jax import lax
from jax.experimental import pallas as pl
from jax.experimental.pallas import tpu as pltpu

F32 = jnp.float32
BF16 = jnp.bfloat16

GRID_W = 64
FOURIER_GROUPS = 4
LRU_CONV_W = 4
LRU_C = 8.0
N_MOD = 6
RMS_EPS = 1e-6
LN_2 = math.log(2.0)
LOG2_E = 1.0 / LN_2
DFT_RADIX = 8

V7X_VMEM_BYTES = 64 * 1024 * 1024
SUBLANES = 8
LANES = 128
PIPE_ROWS = 512


def _params(semantics, vmem_bytes):
    assert vmem_bytes < V7X_VMEM_BYTES, vmem_bytes
    return pltpu.CompilerParams(dimension_semantics=semantics, vmem_limit_bytes=int(vmem_bytes))


def _nbytes(shape, dtype):
    return math.prod(shape) * jnp.dtype(dtype).itemsize


def _sigmoid(x):
    return 0.5 * jnp.tanh(0.5 * x) + 0.5


def _gelu_tanh(x):
    c = math.sqrt(2.0 / math.pi)
    return 0.5 * x * (1.0 + jnp.tanh(c * (x + 0.044715 * (x * x * x))))


GELU_C = math.sqrt(2.0 / math.pi)
GELU_CK = GELU_C * 0.044715


def _twice_gelu_tanh(x, c, ck):
    t = jnp.tanh(x * (c + ck * (x * x)))
    return x + x * t


def _split_bf16(x):
    hi = x.astype(BF16)
    lo = (x - hi.astype(F32)).astype(BF16)
    return hi, lo


def _dot3(a, b):
    a_hi, a_lo = _split_bf16(a)
    b_hi, b_lo = _split_bf16(b)
    d = functools.partial(jnp.dot, preferred_element_type=F32)
    return d(a_hi, b_hi) + (d(a_hi, b_lo) + d(a_lo, b_hi))


def _rms_modulate(x, g, shift, scale):
    ms = jnp.mean(x * x, axis=-1, keepdims=True)
    y = x * lax.rsqrt(ms + RMS_EPS) * g
    return y * (1.0 + scale) + shift


def _mod_kernel(c_ref, w_ref, b_ref, o_ref):
    c = c_ref[...]
    s = c * _sigmoid(c)
    o_ref[...] = _dot3(s, w_ref[...]) + b_ref[...]


def _ada_params(cond, w, b):
    n, d = cond.shape
    width = w.shape[1]
    tn = d
    return pl.pallas_call(
        _mod_kernel,
        out_shape=jax.ShapeDtypeStruct((n, width), F32),
        grid=(width // tn,),
        in_specs=[
            pl.BlockSpec((n, d), lambda j: (0, 0)),
            pl.BlockSpec((d, tn), lambda j: (0, j)),
            pl.BlockSpec((1, tn), lambda j: (0, j)),
        ],
        out_specs=pl.BlockSpec((n, tn), lambda j: (0, j)),
        compiler_params=_params(("parallel",), 6 * _nbytes((d, tn), F32)),
        name="ada_params",
    )(cond, w, b.reshape(1, width))


def _inproj_kernel(x_ref, sh_ref, sc_ref, g_ref, *refs, slab_out):
    n_out = len(refs) // 2
    nb, tp, d = x_ref.shape
    h = _rms_modulate(x_ref[...], g_ref[...], sh_ref[...], sc_ref[...]).reshape(nb * tp, d).astype(BF16)
    for idx, (w_ref, o_ref) in enumerate(zip(refs[:n_out], refs[n_out:])):
        z = jnp.dot(h, w_ref[...], preferred_element_type=F32)
        if idx == slab_out:
            for s in range(o_ref.shape[0]):
                for b in range(nb):
                    o_ref[s, pl.ds(b, tp, stride=nb), :] = z[b * tp:(b + 1) * tp, s * LANES:(s + 1) * LANES]
        else:
            o_ref[...] = z.reshape(nb, tp, z.shape[1]).astype(o_ref.dtype)


def _in_projection(x, mods, mod_rows, norm_g, weights, out_dtypes, slab_out, tp):
    bsz, seq, d = x.shape
    assert bsz == SUBLANES and tp % SUBLANES == 0
    outs, out_specs = [], []
    for idx, (w, dt) in enumerate(zip(weights, out_dtypes)):
        c = w.shape[1]
        if idx == slab_out:
            outs.append(jax.ShapeDtypeStruct((c // LANES, seq * bsz, LANES), dt))
            out_specs.append(pl.BlockSpec((c // LANES, tp * bsz, LANES), lambda i: (0, i, 0)))
        else:
            outs.append(jax.ShapeDtypeStruct((bsz, seq, c), dt))
            out_specs.append(pl.BlockSpec((bsz, tp, c), lambda i: (0, i, 0)))
    w_specs = [pl.BlockSpec(w.shape, lambda i: (0, 0)) for w in weights]
    first, count = mod_rows
    assert first % count == 0
    mod = lambda k: pl.BlockSpec((count, 1, d), lambda i: (first // count, 0, k))
    rows = bsz * tp
    vmem = 2 * (_nbytes((rows, d), F32) + sum(_nbytes(w.shape, w.dtype) for w in weights)
                + sum(_nbytes((rows, w.shape[1]), F32) for w in weights)) + 4 * _nbytes((rows, d), F32)
    return pl.pallas_call(
        functools.partial(_inproj_kernel, slab_out=slab_out),
        out_shape=outs,
        grid=(seq // tp,),
        in_specs=[
            pl.BlockSpec((bsz, tp, d), lambda i: (0, i, 0)),
            mod(0),
            mod(1),
            pl.BlockSpec((1, d), lambda i: (0, 0)),
        ] + w_specs,
        out_specs=out_specs,
        compiler_params=_params(("parallel",), vmem),
        name="in_projection",
    )(x, mods, mods, norm_g.reshape(1, d), *weights)


def _lru_kernel(*refs, reverse, natural_out, steps):
    if natural_out:
        (zp_ref, zm_ref, zn_ref, cw_ref, cb_ref, wg_ref, gab_ref, gxb_ref, lam_ref, h0_ref, hprev_ref,
         o_ref, hl_ref, a_s, b_s, hbuf) = refs
    else:
        (zp_ref, zm_ref, zn_ref, cw_ref, cb_ref, wg_ref, gab_ref, gxb_ref, lam_ref, h0_ref,
         o_ref, hl_ref, a_s, b_s) = refs
    i = pl.program_id(0)
    n = pl.num_programs(0)
    ci = n - 1 - i if reverse else i
    nb = h0_ref.shape[1]
    heads, hd = wg_ref.shape[0], wg_ref.shape[1]
    rows = steps * nb
    halo = SUBLANES * nb

    @pl.when(i == 0)
    def _():
        hl_ref[...] = h0_ref[...]

    left = LRU_CONV_W // 2

    def coeffs(h, carry):
        z = jnp.concatenate([jnp.where(ci > 0, zp_ref[h], 0.0), zm_ref[h],
                             jnp.where(ci < n - 1, zn_ref[h], 0.0)], axis=0)
        u = cb_ref[h]
        for k in range(LRU_CONV_W):
            start = halo + (k - left) * nb
            u = u + cw_ref[h, k:k + 1, :] * z[start:start + rows]
        g = jnp.dot(u.astype(BF16), wg_ref[h], preferred_element_type=F32)
        tr = jnp.tanh(g[:, :hd] + gab_ref[h])
        gi = 0.5 * jnp.tanh(g[:, hd:] + gxb_ref[h]) + 0.5
        nl = -lam_ref[h]
        softplus = jnp.maximum(nl, 0.0) + jnp.log1p(jnp.exp(-jnp.abs(nl)))
        half = (-0.5 * LRU_C * LOG2_E) * softplus
        log2_a = tr * half + half
        a = jnp.exp2(log2_a)
        m2 = jnp.maximum(jnp.tanh(log2_a * (-LN_2)) * (a * a + 1.0), 1e-12)
        mult = m2 * lax.rsqrt(m2)
        a_s[h] = a
        b_s[h] = mult * (gi * u)
        return carry

    lax.fori_loop(0, heads, coeffs, 0)

    def emit(h, r0, v):
        if natural_out:
            hbuf[h, r0:r0 + nb, :] = v + hprev_ref[h, r0:r0 + nb, :]
        else:
            o_ref[h, r0:r0 + nb, :] = v

    hs = [hl_ref[h] for h in range(heads)]
    for i in range(steps // 2):
        t0 = steps - 1 - 2 * i if reverse else 2 * i
        t1 = t0 - 1 if reverse else t0 + 1
        r0, r1 = t0 * nb, t1 * nb
        for h in range(heads):
            a0, b0 = a_s[h, r0:r0 + nb, :], b_s[h, r0:r0 + nb, :]
            a1, b1 = a_s[h, r1:r1 + nb, :], b_s[h, r1:r1 + nb, :]
            h1 = (a1 * a0) * hs[h] + (a1 * b0 + b1)
            emit(h, r0, a0 * hs[h] + b0)
            emit(h, r1, h1)
            hs[h] = h1
    for h in range(heads):
        hl_ref[h] = hs[h]

    if natural_out:
        for h in range(heads):
            for b in range(nb):
                o_ref[b, :, h * hd:(h + 1) * hd] = hbuf[h, pl.ds(b, steps, stride=nb), :]


def _lru_pass(zx, lru_p, h0, h_prev, *, reverse, steps):
    conv_w, conv_b, wg, ga_b, gx_b, lam = lru_p
    heads, total, hd = zx.shape
    nb = h0.shape[1]
    seq, r = total // nb, heads * hd
    assert nb == SUBLANES and hd == LANES and wg.shape[:2] == (heads, hd)
    assert seq % steps == 0 and steps % SUBLANES == 0
    rows = steps * nb
    halo = SUBLANES * nb
    n = seq // steps
    natural_out = h_prev is not None
    chunk = (lambda i: n - 1 - i) if reverse else (lambda i: i)
    per_head = lambda a: a.reshape(-1, heads, hd).transpose(1, 0, 2)
    full = lambda shape: pl.BlockSpec(shape, lambda i: (0,) * len(shape))
    hpc = rows // halo
    in_specs = [
        pl.BlockSpec((heads, halo, hd), lambda i: (0, jnp.maximum(chunk(i) * hpc - 1, 0), 0)),
        pl.BlockSpec((heads, rows, hd), lambda i: (0, chunk(i), 0)),
        pl.BlockSpec((heads, halo, hd), lambda i: (0, jnp.minimum((chunk(i) + 1) * hpc, total // halo - 1), 0)),
        full((heads, LRU_CONV_W, hd)), full((heads, 1, hd)), full(wg.shape), full((heads, 1, hd)),
        full((heads, 1, hd)), full((heads, 1, hd)), full((heads, nb, hd)),
    ]
    args = [zx, zx, zx, per_head(conv_w), per_head(conv_b), wg, per_head(ga_b), per_head(gx_b), per_head(lam), h0]
    slab = pl.BlockSpec((heads, rows, hd), lambda i: (0, chunk(i), 0))
    scratch = [pltpu.VMEM((heads, rows, hd), F32), pltpu.VMEM((heads, rows, hd), F32)]
    if natural_out:
        in_specs.append(slab)
        args.append(h_prev)
        out0 = jax.ShapeDtypeStruct((nb, seq, r), F32)
        out_spec0 = pl.BlockSpec((nb, steps, r), lambda i: (0, chunk(i), 0))
        scratch.append(pltpu.VMEM((heads, rows, hd), F32))
    else:
        out0 = jax.ShapeDtypeStruct((heads, seq * nb, hd), F32)
        out_spec0 = slab
    blk = _nbytes((rows, r), F32)
    vmem = 12 * blk + 2 * _nbytes(wg.shape, wg.dtype)
    return pl.pallas_call(
        functools.partial(_lru_kernel, reverse=reverse, natural_out=natural_out, steps=steps),
        out_shape=[out0, jax.ShapeDtypeStruct((heads, nb, hd), F32)],
        grid=(n,),
        in_specs=in_specs,
        out_specs=[out_spec0, full((heads, nb, hd))],
        scratch_shapes=scratch,
        compiler_params=_params(("arbitrary",), vmem),
        name="lru_reverse" if reverse else "lru_forward",
    )(*args)


def _cmul(re, im, c, s):
    return re * c + im * s, im * c - re * s


def _fft_lists(re, im):
    n = len(re)
    if n == 1:
        return re, im
    er, ei = _fft_lists(re[0::2], im[0::2])
    qr, qi = _fft_lists(re[1::2], im[1::2])
    out_r, out_i = [None] * n, [None] * n
    for k in range(n // 2):
        ang = 2.0 * math.pi * k / n
        if k == 0:
            tr, ti = qr[k], qi[k]
        elif 4 * k == n:
            tr, ti = qi[k], -qr[k]
        else:
            tr, ti = _cmul(qr[k], qi[k], math.cos(ang), math.sin(ang))
        out_r[k], out_i[k] = er[k] + tr, ei[k] + ti
        out_r[k + n // 2], out_i[k + n // 2] = er[k] - tr, ei[k] - ti
    return out_r, out_i


def _dft_kernel(x_ref, w_ref, twc_ref, tws_ref, u_ref, v_ref, a_s):
    rdx = DFT_RADIX
    seq = x_ref.shape[1]
    m = seq // rdx
    w = w_ref[...].astype(BF16)
    for s1 in range(0, rdx, 2):
        xa = x_ref[0, pl.ds(s1, m, stride=rdx), :]
        xb = x_ref[0, pl.ds(s1 + 1, m, stride=rdx), :]
        xx = jnp.concatenate([xa, xb], axis=1).astype(BF16)
        a = jnp.dot(w, xx, preferred_element_type=F32)
        a_s[s1] = a[:, :LANES]
        a_s[s1 + 1] = a[:, LANES:]

    blk = 32
    for r0 in range(0, m, blk):
        re, im = [], []
        for s1 in range(rdx):
            ar = a_s[s1, r0:r0 + blk, :]
            ai = a_s[s1, m + r0:m + r0 + blk, :]
            if s1:
                ar, ai = _cmul(ar, ai, twc_ref[s1, r0:r0 + blk, :], tws_ref[s1, r0:r0 + blk, :])
            re.append(ar)
            im.append(ai)
        yr, yi = _fft_lists(re, im)
        for k1 in range(rdx):
            u_ref[0, k1 * m + r0:k1 * m + r0 + blk, :] = yr[k1].astype(u_ref.dtype)
            v_ref[0, k1 * m + r0:k1 * m + r0 + blk, :] = (-yi[k1]).astype(v_ref.dtype)


def _position_dft(x):
    bsz, seq, c = x.shape
    rdx = DFT_RADIX
    m = seq // rdx
    k = np.arange(m)
    ang = 2.0 * np.pi * np.outer(k, k) / m
    w = jnp.asarray(np.concatenate([np.cos(ang), -np.sin(ang)], axis=0), F32)
    tw = 2.0 * np.pi * np.outer(np.arange(rdx), k) / seq
    bc = lambda t: jnp.asarray(np.broadcast_to(t[:, :, None], (rdx, m, LANES)), F32)
    tok = pl.BlockSpec((1, seq, LANES), lambda b, j: (b, 0, j))
    vmem = 2 * (3 * _nbytes((seq, LANES), F32) + _nbytes((2 * m, m), F32) + 2 * _nbytes((rdx, m, LANES), F32)) \
        + _nbytes((rdx, 2 * m, LANES), F32) + 16 * _nbytes((2 * m, LANES), F32)
    return pl.pallas_call(
        _dft_kernel,
        out_shape=[jax.ShapeDtypeStruct((bsz, seq, c), BF16)] * 2,
        grid=(bsz, c // LANES),
        in_specs=[
            tok,
            pl.BlockSpec((2 * m, m), lambda b, j: (0, 0)),
            pl.BlockSpec((rdx, m, LANES), lambda b, j: (0, 0, 0)),
            pl.BlockSpec((rdx, m, LANES), lambda b, j: (0, 0, 0)),
        ],
        out_specs=[tok, tok],
        scratch_shapes=[pltpu.VMEM((rdx, 2 * m, LANES), F32)],
        compiler_params=_params(("parallel", "parallel"), vmem),
        name="position_dft",
    )(x, w, bc(np.cos(tw)), bc(np.sin(tw)))


def _fold_kernel(cc_ref, sc_ref, w_ref, o_ref):
    w = w_ref[...]
    o_ref[0] = _dot3(cc_ref[...], w).astype(o_ref.dtype)
    o_ref[1] = (-_dot3(sc_ref[...], w)).astype(o_ref.dtype)


def _fold_channel_dft(w_fourier, seq):
    f, d = w_fourier.shape
    gd = f // FOURIER_GROUPS
    k = np.arange(gd)
    ang = 2.0 * np.pi * np.outer(k, k) / gd
    scale = 1.0 / math.sqrt(seq * gd)
    out = pl.pallas_call(
        _fold_kernel,
        out_shape=jax.ShapeDtypeStruct((2, f, d), BF16),
        grid=(FOURIER_GROUPS,),
        in_specs=[
            pl.BlockSpec((gd, gd), lambda g: (0, 0)),
            pl.BlockSpec((gd, gd), lambda g: (0, 0)),
            pl.BlockSpec((gd, d), lambda g: (g, 0)),
        ],
        out_specs=pl.BlockSpec((2, gd, d), lambda g: (0, g, 0)),
        compiler_params=_params(("parallel",), 32 * _nbytes((gd, d), F32)),
        name="fold_channel_dft",
    )(jnp.asarray(np.cos(ang) * scale, F32), jnp.asarray(np.sin(ang) * scale, F32), w_fourier)
    return out.reshape(2 * f, d)


def _merge_kernel(u_ref, v_ref, hs_ref, zy_ref, gf_ref, gr_ref, x_ref, g1_ref, sh2_ref, sc2_ref, n2_ref,
                  wcs_ref, wr_ref, wo_ref, x1_ref, h2_ref):
    uv = jnp.concatenate([u_ref[0], v_ref[0]], axis=1)
    f = jnp.dot(uv, wcs_ref[...], preferred_element_type=F32)
    p = (hs_ref[0] * _gelu_tanh(zy_ref[0].astype(F32))).astype(BF16)
    r = jnp.dot(p, wr_ref[...], preferred_element_type=F32)
    m = _sigmoid(gf_ref[0].astype(F32)) * f + _sigmoid(gr_ref[0].astype(F32)) * r
    o = jnp.dot(m.astype(BF16), wo_ref[...], preferred_element_type=F32)
    x1 = x_ref[0] + g1_ref[0] * o
    x1_ref[0] = x1
    h2_ref[0] = _rms_modulate(x1, n2_ref[...], sh2_ref[0], sc2_ref[0]).astype(h2_ref.dtype)


def _merge(u, v, hsum, zy, zg, x, mods, norm2_g, wcs, w_r, w_o, ts):
    bsz, seq, d = x.shape
    tok = lambda c, k=0: pl.BlockSpec((1, ts, c), lambda b, i: (b, i, k))
    mod = lambda k: pl.BlockSpec((1, 1, d), lambda b, i: (b, 0, k))
    full = lambda a: pl.BlockSpec(a.shape, lambda b, i: (0, 0))
    vmem = 2 * sum(_nbytes(a.shape, a.dtype) for a in (wcs, w_r, w_o)) + 24 * _nbytes((ts, d), F32)
    return pl.pallas_call(
        _merge_kernel,
        out_shape=[jax.ShapeDtypeStruct((bsz, seq, d), F32), jax.ShapeDtypeStruct((bsz, seq, d), BF16)],
        grid=(bsz, seq // ts),
        in_specs=[
            tok(u.shape[2]), tok(v.shape[2]), tok(hsum.shape[2]), tok(zy.shape[2]), tok(d, 0), tok(d, 1),
            tok(d), mod(2), mod(3), mod(4),
            pl.BlockSpec((1, d), lambda b, i: (0, 0)),
            full(wcs), full(w_r), full(w_o),
        ],
        out_specs=[tok(d), tok(d)],
        compiler_params=_params(("parallel", "parallel"), vmem),
        name="merge_mixers",
    )(u, v, hsum, zy, zg, zg, x, mods, mods, mods, norm2_g.reshape(1, d), wcs, w_r, w_o)


def _ffn_kernel(hp_ref, hm_ref, hn_ref, wg_ref, wv_ref, wgn_ref, wvn_ref, cwg_ref, cwv_ref, cbg_ref, cbv_ref,
                wd_ref, x1_ref, g2_ref, fg_ref, gk_ref, o_ref, ua_s, ub_s, p_s):
    part = pl.program_id(1)
    nparts = pl.num_programs(1)
    j = pl.program_id(2)
    nch = pl.num_programs(2)
    rows, d = hm_ref.shape[1], hm_ref.shape[2]
    pairs = wd_ref.shape[0] // LANES
    pad = SUBLANES + GRID_W
    tiles_per_row = GRID_W // SUBLANES
    nblk = rows // PIPE_ROWS
    gpb = PIPE_ROWS // GRID_W
    head = 2
    assert nblk >= head + 2
    sub = lax.broadcasted_iota(jnp.int32, (SUBLANES, LANES), 0)

    def w_pair(wrefs, k):
        ln = slice(k * LANES, (k + 1) * LANES)
        return jnp.concatenate([wrefs[0][:, ln], wrefs[1][:, ln]], axis=1)

    def up_halo(u_s, wrefs):
        for k in range(pairs):
            w = w_pair(wrefs, k)
            top = jnp.where(part > 0, jnp.dot(hp_ref[0], w, preferred_element_type=F32), 0.0)
            bot = jnp.where(part < nparts - 1, jnp.dot(hn_ref[0], w, preferred_element_type=F32), 0.0)
            for gv in range(2):
                u_s[gv * pairs + k, SUBLANES:pad, :] = top[:, gv * LANES:(gv + 1) * LANES]
                u_s[gv * pairs + k, pad + rows:pad + rows + GRID_W, :] = bot[:, gv * LANES:(gv + 1) * LANES]
        return [u_s[k, pad + rows:pad + rows + SUBLANES, :] for k in range(pairs)]

    def up_block(u_s, wrefs, t):
        r0 = t * PIPE_ROWS
        for k in range(pairs):
            u = jnp.dot(hm_ref[0, r0:r0 + PIPE_ROWS, :], w_pair(wrefs, k), preferred_element_type=F32)
            for gv in range(2):
                u_s[gv * pairs + k, pad + r0:pad + r0 + PIPE_ROWS, :] = u[:, gv * LANES:(gv + 1) * LANES]
        return [u_s[k, pad + r0:pad + r0 + SUBLANES, :] for k in range(pairs)]

    def down_block(t):
        r0 = t * PIPE_ROWS
        p = p_s[r0:r0 + PIPE_ROWS, :]
        for n0 in range(0, d, 2 * LANES):
            o_ref[0, r0:r0 + PIPE_ROWS, n0:n0 + 2 * LANES] += jnp.dot(p, wd_ref[:, n0:n0 + 2 * LANES],
                                                                         preferred_element_type=F32)
        return [o_ref[0, r0:r0 + SUBLANES, n0:n0 + LANES] for n0 in range(0, d, 2 * LANES)]

    def zero_after(witnesses):
        bits = pltpu.bitcast(witnesses[0], jnp.uint32)
        for w in witnesses[1:]:
            bits = bits | pltpu.bitcast(w, jnp.uint32)
        half = jnp.uint32(16)
        bits = lax.shift_right_logical(lax.shift_right_logical(bits, half), half)
        return pltpu.bitcast(bits, F32)

    def conv_block(u_s, t, zero):
        cw = (cwg_ref, cwv_ref)
        cb = (cbg_ref, cbv_ref)
        gelu_c = jnp.broadcast_to(gk_ref[0:1, :], (2 * SUBLANES, LANES)).astype(BF16)
        gelu_ck = jnp.broadcast_to(gk_ref[1:2, :], (2 * SUBLANES, LANES)).astype(BF16)
        for k in range(pairs):
            ln = slice(k * LANES, (k + 1) * LANES)
            wt = [[[jnp.broadcast_to(cw[gv][3 * dh + dw:3 * dh + dw + 1, ln], (SUBLANES, LANES))
                    for dw in range(3)] for dh in range(3)] for gv in range(2)]
            bias = [jnp.broadcast_to(cb[gv][:, ln], (SUBLANES, LANES)) for gv in range(2)]
            if zero is not None:
                bias = [b + zero for b in bias]
            for g in range(t * gpb, (t + 1) * gpb):
                g0 = g * GRID_W
                gated = []
                for tile in range(tiles_per_row):
                    base = pad + g0 + tile * SUBLANES
                    acc = []
                    for gv in range(2):
                        cols = []
                        for dw in range(3):
                            col = None
                            for dh in range(3):
                                off = (dh - 1) * GRID_W + (dw - 1)
                                term = wt[gv][dh][dw] * u_s[gv * pairs + k, base + off:base + off + SUBLANES, :]
                                col = term if col is None else col + term
                            cols.append(col)
                        if tile == 0:
                            cols[0] = jnp.where(sub == 0, 0.0, cols[0])
                        if tile == tiles_per_row - 1:
                            cols[2] = jnp.where(sub == SUBLANES - 1, 0.0, cols[2])
                        acc.append((bias[gv] + cols[1]) + (cols[0] + cols[2]))
                    gated.append(acc)
                prod = []
                for q in range(0, tiles_per_row, 2):
                    gate16 = jnp.concatenate([gated[q][0], gated[q + 1][0]], axis=0).astype(BF16)
                    val16 = jnp.concatenate([gated[q][1], gated[q + 1][1]], axis=0).astype(BF16)
                    prod.append(_twice_gelu_tanh(gate16, gelu_c, gelu_ck) * val16)
                p_s[g0:g0 + GRID_W, ln] = jnp.concatenate(prod, axis=0).astype(p_s.dtype)

    cur_w = (wg_ref, wv_ref)
    nxt_w = (wgn_ref, wvn_ref)

    @pl.when(j == 0)
    def _():
        zeros = jnp.zeros((SUBLANES, LANES), F32)
        for u_s in (ua_s, ub_s):
            for s in range(2 * pairs):
                u_s[s, 0:SUBLANES, :] = zeros
                u_s[s, pad + rows + GRID_W:pad + rows + GRID_W + SUBLANES, :] = zeros
        o_ref[...] = jnp.zeros(o_ref.shape, o_ref.dtype)
        up_halo(ua_s, cur_w)
        for t in range(head):
            up_block(ua_s, cur_w, t)

    def step(u_cur, u_nxt):
        ahead = [functools.partial(up_block, u_cur, cur_w, t) for t in range(head, nblk)]
        ahead += [functools.partial(up_halo, u_nxt, nxt_w)]
        ahead += [functools.partial(up_block, u_nxt, nxt_w, t) for t in range(head)]
        per_slot = -(-len(ahead) // (nblk - 1))
        zero = None
        for t in range(nblk):
            witnesses = []
            for fn in ahead[:per_slot]:
                witnesses += fn()
            ahead = ahead[per_slot:]
            if t >= 1:
                witnesses += down_block(t - 1)
            conv_block(u_cur, t, zero)
            zero = zero_after(witnesses) if witnesses else None
        assert not ahead
        down_block(nblk - 1)

    @pl.when(j % 2 == 0)
    def _():
        step(ua_s, ub_s)

    @pl.when(j % 2 == 1)
    def _():
        step(ub_s, ua_s)

    @pl.when(j == nch - 1)
    def _():
        blk = 256

        def finish(t, carry):
            r0 = pl.multiple_of(t * blk, blk)
            x2 = x1_ref[0, pl.ds(r0, blk), :] + g2_ref[0] * o_ref[0, pl.ds(r0, blk), :]
            ms = jnp.mean(x2 * x2, axis=-1, keepdims=True)
            o_ref[0, pl.ds(r0, blk), :] = x2 * lax.rsqrt(ms + RMS_EPS) * fg_ref[...]
            return carry

        lax.fori_loop(0, rows // blk, finish, 0)


def _conv_ffn_final(h2, x1, mods, final_g, w_up, conv_w, conv_b, w_down, *, ck):
    bsz, seq, d = h2.shape
    d_ff = w_down.shape[0]
    nch = d_ff // ck
    pairs = ck // LANES
    nparts = 2
    rows = seq // nparts
    gpr = rows // GRID_W
    cw = conv_w.reshape(9, 2 * d_ff)
    cb = conv_b.reshape(1, 2 * d_ff)
    u_rows = rows + 2 * (GRID_W + SUBLANES)
    gelu_k = jnp.asarray(np.repeat([[GELU_C], [GELU_CK]], LANES, axis=1), F32)
    nxt = lambda j: jnp.minimum(j + 1, nch - 1)
    gate = lambda shape, f=(lambda j: j): pl.BlockSpec(shape, lambda b, p, j: (0, f(j)))
    value = lambda shape, f=(lambda j: j): pl.BlockSpec(shape, lambda b, p, j: (0, nch + f(j)))
    img = pl.BlockSpec((1, rows, d), lambda b, p, j: (b, p, 0))
    img_once = pl.BlockSpec((1, rows, d), lambda b, p, j: (b, p, 0), pipeline_mode=pl.Buffered(1))
    vmem = (_nbytes((rows + 4 * GRID_W, d), BF16) + 4 * _nbytes((rows, d), F32)
            + 2 * _nbytes((2 * pairs, u_rows, LANES), F32) + _nbytes((rows, ck), BF16)
            + 8 * _nbytes((d, ck), BF16) + 2 * _nbytes((ck, d), BF16) + 16 * _nbytes((PIPE_ROWS, 2 * LANES), F32))
    return pl.pallas_call(
        _ffn_kernel,
        out_shape=jax.ShapeDtypeStruct((bsz, seq, d), F32),
        grid=(bsz, nparts, nch),
        in_specs=[
            pl.BlockSpec((1, GRID_W, d), lambda b, p, j: (b, jnp.maximum(p * gpr - 1, 0), 0)),
            img_once,
            pl.BlockSpec((1, GRID_W, d), lambda b, p, j: (b, jnp.minimum((p + 1) * gpr, seq // GRID_W - 1), 0)),
            gate((d, ck)), value((d, ck)), gate((d, ck), nxt), value((d, ck), nxt),
            gate((9, ck)), value((9, ck)), gate((1, ck)), value((1, ck)),
            pl.BlockSpec((ck, d), lambda b, p, j: (j, 0)),
            img,
            pl.BlockSpec((1, 1, d), lambda b, p, j: (b, 0, 5)),
            pl.BlockSpec((1, d), lambda b, p, j: (0, 0)),
            pl.BlockSpec((2, LANES), lambda b, p, j: (0, 0)),
        ],
        out_specs=img,
        scratch_shapes=[
            pltpu.VMEM((2 * pairs, u_rows, LANES), F32),
            pltpu.VMEM((2 * pairs, u_rows, LANES), F32),
            pltpu.VMEM((rows, ck), BF16),
        ],
        compiler_params=_params(("parallel", "parallel", "arbitrary"), vmem),
        name="conv_ffn",
    )(h2, h2, h2, w_up, w_up, w_up, w_up, cw, cw, cb, cb, w_down, x1, mods, final_g.reshape(1, d), gelu_k)


def kernel(x, c, ctx, c_ctx, mod_w, mod_b, norm1_g, norm2_g, w_in, lru_conv_w, lru_conv_b, lru_ga_w, lru_ga_b,
           lru_gx_w, lru_gx_b, lru_lambda, w_fourier, w_lru_out, w_o, ffn_w_up, ffn_conv_w, ffn_conv_b,
           ffn_w_down, final_g):
    bsz, seq, d = x.shape
    assert mod_w.shape[0] == 1 and bsz == SUBLANES
    l = 0
    f = w_fourier.shape[1]
    r = w_lru_out.shape[1]
    heads, hd = lru_ga_w.shape[2], lru_ga_w.shape[3]

    cond = jnp.concatenate([c, c_ctx[None], jnp.zeros((SUBLANES - 1, d), F32)], axis=0)
    mods = _ada_params(cond, mod_w[l], mod_b[l]).reshape(cond.shape[0], 1, N_MOD * d)

    w_in_b = w_in[l].astype(BF16)
    w_f, w_x, w_y, w_g = (w_in_b[:, :f], w_in_b[:, f:f + r], w_in_b[:, f + r:f + 2 * r], w_in_b[:, f + 2 * r:])
    wg = (0.5 * jnp.concatenate([lru_ga_w[l], lru_gx_w[l]], axis=-1)).astype(BF16)
    lru_p = lambda dr: (lru_conv_w[l], lru_conv_b[l][None], wg[dr], 0.5 * lru_ga_b[l, dr][None],
                        0.5 * lru_gx_b[l, dr][None], lru_lambda[l, dr][None])

    (zc_x,) = _in_projection(ctx, mods, (bsz, 1), norm1_g[l], [w_x], [F32], slab_out=0, tp=64)
    zero = jnp.zeros((heads, bsz, hd), F32)
    _, h0_f = _lru_pass(zc_x, lru_p(0), zero, None, reverse=False, steps=64)
    _, h0_b = _lru_pass(zc_x, lru_p(1), zero, None, reverse=True, steps=64)

    z_f, z_x, z_y, z_g = _in_projection(x, mods, (0, bsz), norm1_g[l], [w_f, w_x, w_y, w_g],
                                        [F32, F32, BF16, BF16], slab_out=1, tp=64)
    h_f, _ = _lru_pass(z_x, lru_p(0), h0_f, None, reverse=False, steps=128)
    h_sum, _ = _lru_pass(z_x, lru_p(1), h0_b, h_f, reverse=True, steps=128)
    u, v = _position_dft(z_f)
    wcs = _fold_channel_dft(w_fourier[l], seq)
    x1, h2 = _merge(u, v, h_sum, z_y, z_g, x, mods, norm2_g[l], wcs,
                    w_lru_out[l].astype(BF16), w_o[l].astype(BF16), ts=512)
    return _conv_ffn_final(h2, x1, mods, final_g, ffn_w_up[l].astype(BF16), ffn_conv_w[l], ffn_conv_b[l],
                           (0.5 * ffn_w_down[l]).astype(BF16), ck=256)
```

```python
import functools
import math

import numpy as np
import jax
import jax.numpy as jnp
from jax import lax
from jax.experimental import pallas as pl
from jax.experimental.pallas import tpu as pltpu

F32 = jnp.float32
BF16 = jnp.bfloat16

GRID_W = 64
FOURIER_GROUPS = 4
LRU_CONV_W = 4
LRU_C = 8.0
N_MOD = 6
RMS_EPS = 1e-6
LN_2 = math.log(2.0)
LOG2_E = 1.0 / LN_2
DFT_RADIX = 8

V7X_VMEM_BYTES = 64 * 1024 * 1024
SUBLANES = 8
LANES = 128
PIPE_ROWS = 512


def _params(semantics, vmem_bytes):
    assert vmem_bytes < V7X_VMEM_BYTES, vmem_bytes
    return pltpu.CompilerParams(dimension_semantics=semantics, vmem_limit_bytes=int(vmem_bytes))


def _nbytes(shape, dtype):
    return math.prod(shape) * jnp.dtype(dtype).itemsize


def _sigmoid(x):
    return 0.5 * jnp.tanh(0.5 * x) + 0.5


def _gelu_tanh(x):
    c = math.sqrt(2.0 / math.pi)
    return 0.5 * x * (1.0 + jnp.tanh(c * (x + 0.044715 * (x * x * x))))


GELU_C = math.sqrt(2.0 / math.pi)
GELU_CK = GELU_C * 0.044715


def _twice_gelu_tanh(x, c, ck):
    t = jnp.tanh(x * (c + ck * (x * x)))
    return x + x * t


def _split_bf16(x):
    hi = x.astype(BF16)
    lo = (x - hi.astype(F32)).astype(BF16)
    return hi, lo


def _dot3(a, b):
    a_hi, a_lo = _split_bf16(a)
    b_hi, b_lo = _split_bf16(b)
    d = functools.partial(jnp.dot, preferred_element_type=F32)
    return d(a_hi, b_hi) + (d(a_hi, b_lo) + d(a_lo, b_hi))


def _rms_modulate(x, g, shift, scale):
    ms = jnp.mean(x * x, axis=-1, keepdims=True)
    y = x * lax.rsqrt(ms + RMS_EPS) * g
    return y * (1.0 + scale) + shift


def _mod_kernel(c_ref, w_ref, b_ref, o_ref):
    c = c_ref[...]
    s = c * _sigmoid(c)
    o_ref[...] = _dot3(s, w_ref[...]) + b_ref[...]


def _ada_params(cond, w, b):
    n, d = cond.shape
    width = w.shape[1]
    tn = d
    return pl.pallas_call(
        _mod_kernel,
        out_shape=jax.ShapeDtypeStruct((n, width), F32),
        grid=(width // tn,),
        in_specs=[
            pl.BlockSpec((n, d), lambda j: (0, 0)),
            pl.BlockSpec((d, tn), lambda j: (0, j)),
            pl.BlockSpec((1, tn), lambda j: (0, j)),
        ],
        out_specs=pl.BlockSpec((n, tn), lambda j: (0, j)),
        compiler_params=_params(("parallel",), 6 * _nbytes((d, tn), F32)),
        name="ada_params",
    )(cond, w, b.reshape(1, width))


def _inproj_kernel(x_ref, sh_ref, sc_ref, g_ref, *refs, slab_out):
    n_out = len(refs) // 2
    nb, tp, d = x_ref.shape
    h = _rms_modulate(x_ref[...], g_ref[...], sh_ref[...], sc_ref[...]).reshape(nb * tp, d).astype(BF16)
    for idx, (w_ref, o_ref) in enumerate(zip(refs[:n_out], refs[n_out:])):
        z = jnp.dot(h, w_ref[...], preferred_element_type=F32)
        if idx == slab_out:
            for s in range(o_ref.shape[0]):
                for b in range(nb):
                    o_ref[s, pl.ds(b, tp, stride=nb), :] = z[b * tp:(b + 1) * tp, s * LANES:(s + 1) * LANES]
        else:
            o_ref[...] = z.reshape(nb, tp, z.shape[1]).astype(o_ref.dtype)


def _in_projection(x, mods, mod_rows, norm_g, weights, out_dtypes, slab_out, tp):
    bsz, seq, d = x.shape
    assert bsz == SUBLANES and tp % SUBLANES == 0
    outs, out_specs = [], []
    for idx, (w, dt) in enumerate(zip(weights, out_dtypes)):
        c = w.shape[1]
        if idx == slab_out:
            outs.append(jax.ShapeDtypeStruct((c // LANES, seq * bsz, LANES), dt))
            out_specs.append(pl.BlockSpec((c // LANES, tp * bsz, LANES), lambda i: (0, i, 0)))
        else:
            outs.append(jax.ShapeDtypeStruct((bsz, seq, c), dt))
            out_specs.append(pl.BlockSpec((bsz, tp, c), lambda i: (0, i, 0)))
    w_specs = [pl.BlockSpec(w.shape, lambda i: (0, 0)) for w in weights]
    first, count = mod_rows
    assert first % count == 0
    mod = lambda k: pl.BlockSpec((count, 1, d), lambda i: (first // count, 0, k))
    rows = bsz * tp
    vmem = 2 * (_nbytes((rows, d), F32) + sum(_nbytes(w.shape, w.dtype) for w in weights)
                + sum(_nbytes((rows, w.shape[1]), F32) for w in weights)) + 4 * _nbytes((rows, d), F32)
    return pl.pallas_call(
        functools.partial(_inproj_kernel, slab_out=slab_out),
        out_shape=outs,
        grid=(seq // tp,),
        in_specs=[
            pl.BlockSpec((bsz, tp, d), lambda i: (0, i, 0)),
            mod(0),
            mod(1),
            pl.BlockSpec((1, d), lambda i: (0, 0)),
        ] + w_specs,
        out_specs=out_specs,
        compiler_params=_params(("parallel",), vmem),
        name="in_projection",
    )(x, mods, mods, norm_g.reshape(1, d), *weights)


def _lru_kernel(*refs, reverse, natural_out, steps):
    if natural_out:
        (zp_ref, zm_ref, zn_ref, cw_ref, cb_ref, wg_ref, gab_ref, gxb_ref, lam_ref, h0_ref, hprev_ref,
         o_ref, hl_ref, a_s, b_s, hbuf) = refs
    else:
        (zp_ref, zm_ref, zn_ref, cw_ref, cb_ref, wg_ref, gab_ref, gxb_ref, lam_ref, h0_ref,
         o_ref, hl_ref, a_s, b_s) = refs
    i = pl.program_id(0)
    n = pl.num_programs(0)
    ci = n - 1 - i if reverse else i
    nb = h0_ref.shape[1]
    heads, hd = wg_ref.shape[0], wg_ref.shape[1]
    rows = steps * nb
    halo = SUBLANES * nb

    @pl.when(i == 0)
    def _():
        hl_ref[...] = h0_ref[...]

    left = LRU_CONV_W // 2

    def coeffs(h, carry):
        z = jnp.concatenate([jnp.where(ci > 0, zp_ref[h], 0.0), zm_ref[h],
                             jnp.where(ci < n - 1, zn_ref[h], 0.0)], axis=0)
        u = cb_ref[h]
        for k in range(LRU_CONV_W):
            start = halo + (k - left) * nb
            u = u + cw_ref[h, k:k + 1, :] * z[start:start + rows]
        g = jnp.dot(u.astype(BF16), wg_ref[h], preferred_element_type=F32)
        tr = jnp.tanh(g[:, :hd] + gab_ref[h])
        gi = 0.5 * jnp.tanh(g[:, hd:] + gxb_ref[h]) + 0.5
        nl = -lam_ref[h]
        softplus = jnp.maximum(nl, 0.0) + jnp.log1p(jnp.exp(-jnp.abs(nl)))
        half = (-0.5 * LRU_C * LOG2_E) * softplus
        log2_a = tr * half + half
        a = jnp.exp2(log2_a)
        m2 = jnp.maximum(jnp.tanh(log2_a * (-LN_2)) * (a * a + 1.0), 1e-12)
        mult = m2 * lax.rsqrt(m2)
        a_s[h] = a
        b_s[h] = mult * (gi * u)
        return carry

    lax.fori_loop(0, heads, coeffs, 0)

    def emit(h, r0, v):
        if natural_out:
            hbuf[h, r0:r0 + nb, :] = v + hprev_ref[h, r0:r0 + nb, :]
        else:
            o_ref[h, r0:r0 + nb, :] = v

    hs = [hl_ref[h] for h in range(heads)]
    for i in range(steps // 2):
        t0 = steps - 1 - 2 * i if reverse else 2 * i
        t1 = t0 - 1 if reverse else t0 + 1
        r0, r1 = t0 * nb, t1 * nb
        for h in range(heads):
            a0, b0 = a_s[h, r0:r0 + nb, :], b_s[h, r0:r0 + nb, :]
            a1, b1 = a_s[h, r1:r1 + nb, :], b_s[h, r1:r1 + nb, :]
            h1 = (a1 * a0) * hs[h] + (a1 * b0 + b1)
            emit(h, r0, a0 * hs[h] + b0)
            emit(h, r1, h1)
            hs[h] = h1
    for h in range(heads):
        hl_ref[h] = hs[h]

    if natural_out:
        for h in range(heads):
            for b in range(nb):
                o_ref[b, :, h * hd:(h + 1) * hd] = hbuf[h, pl.ds(b, steps, stride=nb), :]


def _lru_pass(zx, lru_p, h0, h_prev, *, reverse, steps):
    conv_w, conv_b, wg, ga_b, gx_b, lam = lru_p
    heads, total, hd = zx.shape
    nb = h0.shape[1]
    seq, r = total // nb, heads * hd
    assert nb == SUBLANES and hd == LANES and wg.shape[:2] == (heads, hd)
    assert seq % steps == 0 and steps % SUBLANES == 0
    rows = steps * nb
    halo = SUBLANES * nb
    n = seq // steps
    natural_out = h_prev is not None
    chunk = (lambda i: n - 1 - i) if reverse else (lambda i: i)
    per_head = lambda a: a.reshape(-1, heads, hd).transpose(1, 0, 2)
    full = lambda shape: pl.BlockSpec(shape, lambda i: (0,) * len(shape))
    hpc = rows // halo
    in_specs = [
        pl.BlockSpec((heads, halo, hd), lambda i: (0, jnp.maximum(chunk(i) * hpc - 1, 0), 0)),
        pl.BlockSpec((heads, rows, hd), lambda i: (0, chunk(i), 0)),
        pl.BlockSpec((heads, halo, hd), lambda i: (0, jnp.minimum((chunk(i) + 1) * hpc, total // halo - 1), 0)),
        full((heads, LRU_CONV_W, hd)), full((heads, 1, hd)), full(wg.shape), full((heads, 1, hd)),
        full((heads, 1, hd)), full((heads, 1, hd)), full((heads, nb, hd)),
    ]
    args = [zx, zx, zx, per_head(conv_w), per_head(conv_b), wg, per_head(ga_b), per_head(gx_b), per_head(lam), h0]
    slab = pl.BlockSpec((heads, rows, hd), lambda i: (0, chunk(i), 0))
    scratch = [pltpu.VMEM((heads, rows, hd), F32), pltpu.VMEM((heads, rows, hd), F32)]
    if natural_out:
        in_specs.append(slab)
        args.append(h_prev)
        out0 = jax.ShapeDtypeStruct((nb, seq, r), F32)
        out_spec0 = pl.BlockSpec((nb, steps, r), lambda i: (0, chunk(i), 0))
        scratch.append(pltpu.VMEM((heads, rows, hd), F32))
    else:
        out0 = jax.ShapeDtypeStruct((heads, seq * nb, hd), F32)
        out_spec0 = slab
    blk = _nbytes((rows, r), F32)
    vmem = 12 * blk + 2 * _nbytes(wg.shape, wg.dtype)
    return pl.pallas_call(
        functools.partial(_lru_kernel, reverse=reverse, natural_out=natural_out, steps=steps),
        out_shape=[out0, jax.ShapeDtypeStruct((heads, nb, hd), F32)],
        grid=(n,),
        in_specs=in_specs,
        out_specs=[out_spec0, full((heads, nb, hd))],
        scratch_shapes=scratch,
        compiler_params=_params(("arbitrary",), vmem),
        name="lru_reverse" if reverse else "lru_forward",
    )(*args)


def _cmul(re, im, c, s):
    return re * c + im * s, im * c - re * s


def _fft_lists(re, im):
    n = len(re)
    if n == 1:
        return re, im
    er, ei = _fft_lists(re[0::2], im[0::2])
    qr, qi = _fft_lists(re[1::2], im[1::2])
    out_r, out_i = [None] * n, [None] * n
    for k in range(n // 2):
        ang = 2.0 * math.pi * k / n
        if k == 0:
            tr, ti = qr[k], qi[k]
        elif 4 * k == n:
            tr, ti = qi[k], -qr[k]
        else:
            tr, ti = _cmul(qr[k], qi[k], math.cos(ang), math.sin(ang))
        out_r[k], out_i[k] = er[k] + tr, ei[k] + ti
        out_r[k + n // 2], out_i[k + n // 2] = er[k] - tr, ei[k] - ti
    return out_r, out_i


def _dft_kernel(x_ref, w_ref, twc_ref, tws_ref, u_ref, v_ref, a_s):
    rdx = DFT_RADIX
    seq = x_ref.shape[1]
    m = seq // rdx
    w = w_ref[...].astype(BF16)
    for s1 in range(0, rdx, 2):
        xa = x_ref[0, pl.ds(s1, m, stride=rdx), :]
        xb = x_ref[0, pl.ds(s1 + 1, m, stride=rdx), :]
        xx = jnp.concatenate([xa, xb], axis=1).astype(BF16)
        a = jnp.dot(w, xx, preferred_element_type=F32)
        a_s[s1] = a[:, :LANES]
        a_s[s1 + 1] = a[:, LANES:]

    blk = 32
    for r0 in range(0, m, blk):
        re, im = [], []
        for s1 in range(rdx):
            ar = a_s[s1, r0:r0 + blk, :]
            ai = a_s[s1, m + r0:m + r0 + blk, :]
            if s1:
                ar, ai = _cmul(ar, ai, twc_ref[s1, r0:r0 + blk, :], tws_ref[s1, r0:r0 + blk, :])
            re.append(ar)
            im.append(ai)
        yr, yi = _fft_lists(re, im)
        for k1 in range(rdx):
            u_ref[0, k1 * m + r0:k1 * m + r0 + blk, :] = yr[k1].astype(u_ref.dtype)
            v_ref[0, k1 * m + r0:k1 * m + r0 + blk, :] = (-yi[k1]).astype(v_ref.dtype)


def _position_dft(x):
    bsz, seq, c = x.shape
    rdx = DFT_RADIX
    m = seq // rdx
    k = np.arange(m)
    ang = 2.0 * np.pi * np.outer(k, k) / m
    w = jnp.asarray(np.concatenate([np.cos(ang), -np.sin(ang)], axis=0), F32)
    tw = 2.0 * np.pi * np.outer(np.arange(rdx), k) / seq
    bc = lambda t: jnp.asarray(np.broadcast_to(t[:, :, None], (rdx, m, LANES)), F32)
    tok = pl.BlockSpec((1, seq, LANES), lambda b, j: (b, 0, j))
    vmem = 2 * (3 * _nbytes((seq, LANES), F32) + _nbytes((2 * m, m), F32) + 2 * _nbytes((rdx, m, LANES), F32)) \
        + _nbytes((rdx, 2 * m, LANES), F32) + 16 * _nbytes((2 * m, LANES), F32)
    return pl.pallas_call(
        _dft_kernel,
        out_shape=[jax.ShapeDtypeStruct((bsz, seq, c), BF16)] * 2,
        grid=(bsz, c // LANES),
        in_specs=[
            tok,
            pl.BlockSpec((2 * m, m), lambda b, j: (0, 0)),
            pl.BlockSpec((rdx, m, LANES), lambda b, j: (0, 0, 0)),
            pl.BlockSpec((rdx, m, LANES), lambda b, j: (0, 0, 0)),
        ],
        out_specs=[tok, tok],
        scratch_shapes=[pltpu.VMEM((rdx, 2 * m, LANES), F32)],
        compiler_params=_params(("parallel", "parallel"), vmem),
        name="position_dft",
    )(x, w, bc(np.cos(tw)), bc(np.sin(tw)))


def _fold_kernel(cc_ref, sc_ref, w_ref, o_ref):
    w = w_ref[...]
    o_ref[0] = _dot3(cc_ref[...], w).astype(o_ref.dtype)
    o_ref[1] = (-_dot3(sc_ref[...], w)).astype(o_ref.dtype)


def _fold_channel_dft(w_fourier, seq):
    f, d = w_fourier.shape
    gd = f // FOURIER_GROUPS
    k = np.arange(gd)
    ang = 2.0 * np.pi * np.outer(k, k) / gd
    scale = 1.0 / math.sqrt(seq * gd)
    out = pl.pallas_call(
        _fold_kernel,
        out_shape=jax.ShapeDtypeStruct((2, f, d), BF16),
        grid=(FOURIER_GROUPS,),
        in_specs=[
            pl.BlockSpec((gd, gd), lambda g: (0, 0)),
            pl.BlockSpec((gd, gd), lambda g: (0, 0)),
            pl.BlockSpec((gd, d), lambda g: (g, 0)),
        ],
        out_specs=pl.BlockSpec((2, gd, d), lambda g: (0, g, 0)),
        compiler_params=_params(("parallel",), 32 * _nbytes((gd, d), F32)),
        name="fold_channel_dft",
    )(jnp.asarray(np.cos(ang) * scale, F32), jnp.asarray(np.sin(ang) * scale, F32), w_fourier)
    return out.reshape(2 * f, d)


def _merge_kernel(u_ref, v_ref, hs_ref, zy_ref, gf_ref, gr_ref, x_ref, g1_ref, sh2_ref, sc2_ref, n2_ref,
                  wcs_ref, wr_ref, wo_ref, x1_ref, h2_ref):
    uv = jnp.concatenate([u_ref[0], v_ref[0]], axis=1)
    f = jnp.dot(uv, wcs_ref[...], preferred_element_type=F32)
    p = (hs_ref[0] * _gelu_tanh(zy_ref[0].astype(F32))).astype(BF16)
    r = jnp.dot(p, wr_ref[...], preferred_element_type=F32)
    m = _sigmoid(gf_ref[0].astype(F32)) * f + _sigmoid(gr_ref[0].astype(F32)) * r
    o = jnp.dot(m.astype(BF16), wo_ref[...], preferred_element_type=F32)
    x1 = x_ref[0] + g1_ref[0] * o
    x1_ref[0] = x1
    h2_ref[0] = _rms_modulate(x1, n2_ref[...], sh2_ref[0], sc2_ref[0]).astype(h2_ref.dtype)


def _merge(u, v, hsum, zy, zg, x, mods, norm2_g, wcs, w_r, w_o, ts):
    bsz, seq, d = x.shape
    tok = lambda c, k=0: pl.BlockSpec((1, ts, c), lambda b, i: (b, i, k))
    mod = lambda k: pl.BlockSpec((1, 1, d), lambda b, i: (b, 0, k))
    full = lambda a: pl.BlockSpec(a.shape, lambda b, i: (0, 0))
    vmem = 2 * sum(_nbytes(a.shape, a.dtype) for a in (wcs, w_r, w_o)) + 24 * _nbytes((ts, d), F32)
    return pl.pallas_call(
        _merge_kernel,
        out_shape=[jax.ShapeDtypeStruct((bsz, seq, d), F32), jax.ShapeDtypeStruct((bsz, seq, d), BF16)],
        grid=(bsz, seq // ts),
        in_specs=[
            tok(u.shape[2]), tok(v.shape[2]), tok(hsum.shape[2]), tok(zy.shape[2]), tok(d, 0), tok(d, 1),
            tok(d), mod(2), mod(3), mod(4),
            pl.BlockSpec((1, d), lambda b, i: (0, 0)),
            full(wcs), full(w_r), full(w_o),
        ],
        out_specs=[tok(d), tok(d)],
        compiler_params=_params(("parallel", "parallel"), vmem),
        name="merge_mixers",
    )(u, v, hsum, zy, zg, zg, x, mods, mods, mods, norm2_g.reshape(1, d), wcs, w_r, w_o)


def _ffn_kernel(hp_ref, hm_ref, hn_ref, wg_ref, wv_ref, wgn_ref, wvn_ref, cwg_ref, cwv_ref, cbg_ref, cbv_ref,
                wd_ref, x1_ref, g2_ref, fg_ref, o_ref, ua_s, ub_s, p_s):
    part = pl.program_id(1)
    nparts = pl.num_programs(1)
    j = pl.program_id(2)
    nch = pl.num_programs(2)
    rows, d = hm_ref.shape[1], hm_ref.shape[2]
    pairs = wd_ref.shape[0] // LANES
    pad = SUBLANES + GRID_W
    tiles_per_row = GRID_W // SUBLANES
    nblk = rows // PIPE_ROWS
    gpb = PIPE_ROWS // GRID_W
    head = 2
    assert nblk >= head + 2
    sub = lax.broadcasted_iota(jnp.int32, (SUBLANES, LANES), 0)

    def w_pair(wrefs, k):
        ln = slice(k * LANES, (k + 1) * LANES)
        return jnp.concatenate([wrefs[0][:, ln], wrefs[1][:, ln]], axis=1)

    def up_halo(u_s, wrefs):
        for k in range(pairs):
            w = w_pair(wrefs, k)
            top = jnp.where(part > 0, jnp.dot(hp_ref[0], w, preferred_element_type=F32), 0.0)
            bot = jnp.where(part < nparts - 1, jnp.dot(hn_ref[0], w, preferred_element_type=F32), 0.0)
            for gv in range(2):
                u_s[gv * pairs + k, SUBLANES:pad, :] = top[:, gv * LANES:(gv + 1) * LANES]
                u_s[gv * pairs + k, pad + rows:pad + rows + GRID_W, :] = bot[:, gv * LANES:(gv + 1) * LANES]
        return [u_s[k, pad + rows:pad + rows + SUBLANES, :] for k in range(pairs)]

    def up_block(u_s, wrefs, t):
        r0 = t * PIPE_ROWS
        for k in range(pairs):
            u = jnp.dot(hm_ref[0, r0:r0 + PIPE_ROWS, :], w_pair(wrefs, k), preferred_element_type=F32)
            for gv in range(2):
                u_s[gv * pairs + k, pad + r0:pad + r0 + PIPE_ROWS, :] = u[:, gv * LANES:(gv + 1) * LANES]
        return [u_s[k, pad + r0:pad + r0 + SUBLANES, :] for k in range(pairs)]

    def down_block(t):
        r0 = t * PIPE_ROWS
        p = p_s[r0:r0 + PIPE_ROWS, :]
        for n0 in range(0, d, 2 * LANES):
            o_ref[0, r0:r0 + PIPE_ROWS, n0:n0 + 2 * LANES] += jnp.dot(p, wd_ref[:, n0:n0 + 2 * LANES],
                                                                         preferred_element_type=F32)
        return [o_ref[0, r0:r0 + SUBLANES, n0:n0 + LANES] for n0 in range(0, d, 2 * LANES)]

    def zero_after(witnesses):
        bits = pltpu.bitcast(witnesses[0], jnp.uint32)
        for w in witnesses[1:]:
            bits = bits | pltpu.bitcast(w, jnp.uint32)
        half = jnp.uint32(16)
        bits = lax.shift_right_logical(lax.shift_right_logical(bits, half), half)
        return pltpu.bitcast(bits, F32)

    def conv_block(u_s, t, zero):
        cw = (cwg_ref, cwv_ref)
        cb = (cbg_ref, cbv_ref)
        for k in range(pairs):
            ln = slice(k * LANES, (k + 1) * LANES)
            wt = [[[jnp.broadcast_to(cw[gv][3 * dh + dw:3 * dh + dw + 1, ln], (SUBLANES, LANES))
                    for dw in range(3)] for dh in range(3)] for gv in range(2)]
            bias = [jnp.broadcast_to(cb[gv][:, ln], (SUBLANES, LANES)) for gv in range(2)]
            if zero is not None:
                bias = [b + zero for b in bias]
            for g in range(t * gpb, (t + 1) * gpb):
                g0 = g * GRID_W
                gated = []
                for tile in range(tiles_per_row):
                    base = pad + g0 + tile * SUBLANES
                    acc = []
                    for gv in range(2):
                        cols = []
                        for dw in range(3):
                            col = None
                            for dh in range(3):
                                off = (dh - 1) * GRID_W + (dw - 1)
                                term = wt[gv][dh][dw] * u_s[gv * pairs + k, base + off:base + off + SUBLANES, :]
                                col = term if col is None else col + term
                            cols.append(col)
                        if tile == 0:
                            cols[0] = jnp.where(sub == 0, 0.0, cols[0])
                        if tile == tiles_per_row - 1:
                            cols[2] = jnp.where(sub == SUBLANES - 1, 0.0, cols[2])
                        acc.append((bias[gv] + cols[1]) + (cols[0] + cols[2]))
                    gated.append(_twice_gelu_tanh(acc[0], GELU_C, GELU_CK) * acc[1])
                p_s[g0:g0 + GRID_W, ln] = jnp.concatenate(gated, axis=0).astype(p_s.dtype)

    cur_w = (wg_ref, wv_ref)
    nxt_w = (wgn_ref, wvn_ref)

    @pl.when(j == 0)
    def _():
        zeros = jnp.zeros((SUBLANES, LANES), F32)
        for u_s in (ua_s, ub_s):
            for s in range(2 * pairs):
                u_s[s, 0:SUBLANES, :] = zeros
                u_s[s, pad + rows + GRID_W:pad + rows + GRID_W + SUBLANES, :] = zeros
        o_ref[...] = jnp.zeros(o_ref.shape, o_ref.dtype)
        up_halo(ua_s, cur_w)
        for t in range(head):
            up_block(ua_s, cur_w, t)

    def step(u_cur, u_nxt):
        ahead = [functools.partial(up_block, u_cur, cur_w, t) for t in range(head, nblk)]
        ahead += [functools.partial(up_halo, u_nxt, nxt_w)]
        ahead += [functools.partial(up_block, u_nxt, nxt_w, t) for t in range(head)]
        per_slot = -(-len(ahead) // (nblk - 1))
        zero = None
        for t in range(nblk):
            witnesses = []
            for fn in ahead[:per_slot]:
                witnesses += fn()
            ahead = ahead[per_slot:]
            if t >= 1:
                witnesses += down_block(t - 1)
            conv_block(u_cur, t, zero)
            zero = zero_after(witnesses) if witnesses else None
        assert not ahead
        down_block(nblk - 1)

    @pl.when(j % 2 == 0)
    def _():
        step(ua_s, ub_s)

    @pl.when(j % 2 == 1)
    def _():
        step(ub_s, ua_s)

    @pl.when(j == nch - 1)
    def _():
        blk = 256

        def finish(t, carry):
            r0 = pl.multiple_of(t * blk, blk)
            x2 = x1_ref[0, pl.ds(r0, blk), :] + g2_ref[0] * o_ref[0, pl.ds(r0, blk), :]
            ms = jnp.mean(x2 * x2, axis=-1, keepdims=True)
            o_ref[0, pl.ds(r0, blk), :] = x2 * lax.rsqrt(ms + RMS_EPS) * fg_ref[...]
            return carry

        lax.fori_loop(0, rows // blk, finish, 0)


def _conv_ffn_final(h2, x1, mods, final_g, w_up, conv_w, conv_b, w_down, *, ck):
    bsz, seq, d = h2.shape
    d_ff = w_down.shape[0]
    nch = d_ff // ck
    pairs = ck // LANES
    nparts = 2
    rows = seq // nparts
    gpr = rows // GRID_W
    cw = conv_w.reshape(9, 2 * d_ff)
    cb = conv_b.reshape(1, 2 * d_ff)
    u_rows = rows + 2 * (GRID_W + SUBLANES)
    nxt = lambda j: jnp.minimum(j + 1, nch - 1)
    gate = lambda shape, f=(lambda j: j): pl.BlockSpec(shape, lambda b, p, j: (0, f(j)))
    value = lambda shape, f=(lambda j: j): pl.BlockSpec(shape, lambda b, p, j: (0, nch + f(j)))
    img = pl.BlockSpec((1, rows, d), lambda b, p, j: (b, p, 0))
    img_once = pl.BlockSpec((1, rows, d), lambda b, p, j: (b, p, 0), pipeline_mode=pl.Buffered(1))
    vmem = (_nbytes((rows + 4 * GRID_W, d), BF16) + 4 * _nbytes((rows, d), F32)
            + 2 * _nbytes((2 * pairs, u_rows, LANES), F32) + _nbytes((rows, ck), BF16)
            + 8 * _nbytes((d, ck), BF16) + 2 * _nbytes((ck, d), BF16) + 16 * _nbytes((PIPE_ROWS, 2 * LANES), F32))
    return pl.pallas_call(
        _ffn_kernel,
        out_shape=jax.ShapeDtypeStruct((bsz, seq, d), F32),
        grid=(bsz, nparts, nch),
        in_specs=[
            pl.BlockSpec((1, GRID_W, d), lambda b, p, j: (b, jnp.maximum(p * gpr - 1, 0), 0)),
            img_once,
            pl.BlockSpec((1, GRID_W, d), lambda b, p, j: (b, jnp.minimum((p + 1) * gpr, seq // GRID_W - 1), 0)),
            gate((d, ck)), value((d, ck)), gate((d, ck), nxt), value((d, ck), nxt),
            gate((9, ck)), value((9, ck)), gate((1, ck)), value((1, ck)),
            pl.BlockSpec((ck, d), lambda b, p, j: (j, 0)),
            img,
            pl.BlockSpec((1, 1, d), lambda b, p, j: (b, 0, 5)),
            pl.BlockSpec((1, d), lambda b, p, j: (0, 0)),
        ],
        out_specs=img,
        scratch_shapes=[
            pltpu.VMEM((2 * pairs, u_rows, LANES), F32),
            pltpu.VMEM((2 * pairs, u_rows, LANES), F32),
            pltpu.VMEM((rows, ck), BF16),
        ],
        compiler_params=_params(("parallel", "parallel", "arbitrary"), vmem),
        name="conv_ffn",
    )(h2, h2, h2, w_up, w_up, w_up, w_up, cw, cw, cb, cb, w_down, x1, mods, final_g.reshape(1, d))


def kernel(x, c, ctx, c_ctx, mod_w, mod_b, norm1_g, norm2_g, w_in, lru_conv_w, lru_conv_b, lru_ga_w, lru_ga_b,
           lru_gx_w, lru_gx_b, lru_lambda, w_fourier, w_lru_out, w_o, ffn_w_up, ffn_conv_w, ffn_conv_b,
           ffn_w_down, final_g):
    bsz, seq, d = x.shape
    assert mod_w.shape[0] == 1 and bsz == SUBLANES
    l = 0
    f = w_fourier.shape[1]
    r = w_lru_out.shape[1]
    heads, hd = lru_ga_w.shape[2], lru_ga_w.shape[3]

    cond = jnp.concatenate([c, c_ctx[None], jnp.zeros((SUBLANES - 1, d), F32)], axis=0)
    mods = _ada_params(cond, mod_w[l], mod_b[l]).reshape(cond.shape[0], 1, N_MOD * d)

    w_in_b = w_in[l].astype(BF16)
    w_f, w_x, w_y, w_g = (w_in_b[:, :f], w_in_b[:, f:f + r], w_in_b[:, f + r:f + 2 * r], w_in_b[:, f + 2 * r:])
    wg = (0.5 * jnp.concatenate([lru_ga_w[l], lru_gx_w[l]], axis=-1)).astype(BF16)
    lru_p = lambda dr: (lru_conv_w[l], lru_conv_b[l][None], wg[dr], 0.5 * lru_ga_b[l, dr][None],
                        0.5 * lru_gx_b[l, dr][None], lru_lambda[l, dr][None])

    (zc_x,) = _in_projection(ctx, mods, (bsz, 1), norm1_g[l], [w_x], [F32], slab_out=0, tp=64)
    zero = jnp.zeros((heads, bsz, hd), F32)
    _, h0_f = _lru_pass(zc_x, lru_p(0), zero, None, reverse=False, steps=64)
    _, h0_b = _lru_pass(zc_x, lru_p(1), zero, None, reverse=True, steps=64)

    z_f, z_x, z_y, z_g = _in_projection(x, mods, (0, bsz), norm1_g[l], [w_f, w_x, w_y, w_g],
                                        [F32, F32, BF16, BF16], slab_out=1, tp=64)
    h_f, _ = _lru_pass(z_x, lru_p(0), h0_f, None, reverse=False, steps=128)
    h_sum, _ = _lru_pass(z_x, lru_p(1), h0_b, h_f, reverse=True, steps=128)
    u, v = _position_dft(z_f)
    wcs = _fold_channel_dft(w_fourier[l], seq)
    x1, h2 = _merge(u, v, h_sum, z_y, z_g, x, mods, norm2_g[l], wcs,
                    w_lru_out[l].astype(BF16), w_o[l].astype(BF16), ts=512)
    return _conv_ffn_final(h2, x1, mods, final_g, ffn_w_up[l].astype(BF16), ffn_conv_w[l], ffn_conv_b[l],
                           (0.5 * ffn_w_down[l]).astype(BF16), ck=256)
```
